```python
import math
import jax, jax.numpy as jnp
from jax import lax
import numpy as np

D_MODEL = 4096
BATCH = 4
SEQ = 2048
DEPTH = 4
DEC_BATCH = 128
DEC_SEQ = 8
PAST_LEN = 16384
PAGE_SIZE = 128

N_MIXERS = 3
N_RET = (DEPTH + 2) // 3
N_SSD = (DEPTH + 1) // 3
N_ML = DEPTH // 3
CHUNK = 64
EPS = 1e-6
D_FF = 4 * D_MODEL

RET_HEADS = 16
RET_DK = D_MODEL // RET_HEADS
RET_DV = D_MODEL // RET_HEADS
ROPE_BASE = 10000.0

SSD_DINNER = 2 * D_MODEL
SSD_HEADDIM = 64
SSD_HEADS = SSD_DINNER // SSD_HEADDIM
SSD_STATE = 128
SSD_GROUPS = 8
SSD_REP = SSD_HEADS // SSD_GROUPS
SSD_CONV = 4
SSD_GN = SSD_GROUPS * SSD_STATE
SSD_CONV_DIM = SSD_DINNER + 2 * SSD_GN
SSD_PROJ = SSD_DINNER + SSD_CONV_DIM + SSD_HEADS

ML_HEADS = 8
ML_DQK = D_MODEL // (2 * ML_HEADS)
ML_DV = D_MODEL // ML_HEADS
ML_QK = ML_HEADS * ML_DQK
ML_PROJ = 2 * ML_QK + 2 * D_MODEL + 2 * ML_HEADS

kernel_name = "hybrid_retention_ssd_mlstm_decoder_step"

F32 = jnp.float32


def rmsnorm(x, w):
    xf = x.astype(F32)
    y = xf * lax.rsqrt(jnp.mean(xf * xf, -1, keepdims=True) + EPS)
    return (y * w.astype(F32)).astype(x.dtype)


def chunk_len(l):
    return CHUNK if l % CHUNK == 0 else l


def to_chunks(a, q):
    b, l = a.shape[:2]
    return jnp.moveaxis(a.reshape(b, l // q, q, *a.shape[2:]), 1, 0)


def from_chunks(a):
    n, b, q = a.shape[:3]
    return jnp.moveaxis(a, 0, 1).reshape(b, n * q, *a.shape[3:])


def rotary(x, pos0):
    l, d = x.shape[1], x.shape[-1]
    half = d // 2
    inv = jnp.power(ROPE_BASE, -jnp.arange(half, dtype=F32) / half)
    ang = (pos0 + jnp.arange(l, dtype=F32))[:, None] * inv[None, :]
    cos = jnp.cos(ang)[None, :, None, :]
    sin = jnp.sin(ang)[None, :, None, :]
    xf = x.astype(F32)
    x1, x2 = xf[..., :half], xf[..., half:]
    return jnp.concatenate([x1 * cos - x2 * sin, x2 * cos + x1 * sin], -1).astype(x.dtype)


def retention_mixer(h, pos0, state, w_in, w_out):
    b, l, _ = h.shape
    q, k, v, g = jnp.split(h @ w_in, 4, axis=-1)
    q = rotary(q.reshape(b, l, RET_HEADS, RET_DK), pos0)
    k = rotary(k.reshape(b, l, RET_HEADS, RET_DK), pos0) * (RET_DK ** -0.5)
    v = v.reshape(b, l, RET_HEADS, RET_DV)
    qn = chunk_len(l)
    idx = jnp.arange(qn, dtype=F32)
    diff = idx[:, None] - idx[None, :]
    logg = jnp.log1p(-jnp.exp2(-5.0 - jnp.arange(RET_HEADS, dtype=F32)))
    intra = jnp.where(diff[None] >= 0, jnp.exp(logg[:, None, None] * jnp.maximum(diff, 0.0)[None]), 0.0)
    cross = jnp.exp(logg[None, :] * (idx[:, None] + 1.0))
    into = jnp.exp(logg[None, :] * (qn - 1.0 - idx[:, None]))
    whole = jnp.exp(logg * qn)

    def step(s, inp):
        qc, kc, vc = (a.astype(F32) for a in inp)
        sc = jnp.einsum('bqhd,bkhd->bhqk', qc, kc) * intra[None]
        o = (jnp.einsum('bhqk,bkhe->bqhe', sc, vc)
             + jnp.einsum('bqhd,bhde->bqhe', qc, s) * cross[None, :, :, None])
        s = s * whole[None, :, None, None] + jnp.einsum('bkhd,bkhe->bhde', kc * into[None, :, :, None], vc)
        return s, o

    s, o = lax.scan(step, state.astype(F32), (to_chunks(q, qn), to_chunks(k, qn), to_chunks(v, qn)))
    o = from_chunks(o)
    mu = jnp.mean(o, -1, keepdims=True)
    var = jnp.mean(jnp.square(o - mu), -1, keepdims=True)
    o = ((o - mu) * lax.rsqrt(var + EPS)).reshape(b, l, D_MODEL).astype(h.dtype)
    return (jax.nn.silu(g) * o) @ w_out, s.astype(state.dtype)


def ssd_mixer(h, ssm_state, conv_state, w_in, conv_w, conv_b, dt_bias, a_log, d_skip, norm_w, w_out):
    b, l, _ = h.shape
    proj = h @ w_in
    z = proj[..., :SSD_DINNER]
    xbc = proj[..., SSD_DINNER:SSD_DINNER + SSD_CONV_DIM]
    dt = proj[..., SSD_DINNER + SSD_CONV_DIM:]
    full = jnp.concatenate([conv_state.astype(xbc.dtype), xbc], axis=1)
    new_conv = full[:, full.shape[1] - (SSD_CONV - 1):]
    acc = conv_b
    for w in range(SSD_CONV):
        acc = acc + full[:, w:w + l] * conv_w[w]
    xbc = jax.nn.silu(acc)
    xs = xbc[..., :SSD_DINNER].reshape(b, l, SSD_HEADS, SSD_HEADDIM)
    bm = xbc[..., SSD_DINNER:SSD_DINNER + SSD_GN].reshape(b, l, SSD_GROUPS, SSD_STATE)
    cm = xbc[..., SSD_DINNER + SSD_GN:].reshape(b, l, SSD_GROUPS, SSD_STATE)
    dt = jax.nn.softplus(dt.astype(F32) + dt_bias.astype(F32))
    la = dt * (-jnp.exp(a_log.astype(F32)))
    qn = chunk_len(l)
    causal = jnp.tril(jnp.ones((qn, qn), bool))[None, :, :, None]

    def step(s, inp):
        xc, bc, cc, dtc, lac = inp
        xc, bc, cc = xc.astype(F32), bc.astype(F32), cc.astype(F32)
        cum = jnp.cumsum(lac, axis=1)
        seg = cum[:, :, None, :] - cum[:, None, :, :]
        dec = jnp.where(causal, jnp.exp(jnp.where(causal, seg, 0.0)), 0.0)
        cb = jnp.repeat(jnp.einsum('bign,bjgn->bijg', cc, bc), SSD_REP, axis=3)
        xdt = xc * dtc[..., None]
        y = jnp.einsum('bijh,bjhp->bihp', cb * dec, xdt)
        ch = jnp.repeat(cc, SSD_REP, axis=2)
        bh = jnp.repeat(bc, SSD_REP, axis=2)
        y = y + jnp.einsum('bihn,bhpn->bihp', ch, s) * jnp.exp(cum)[..., None]
        last = cum[:, -1]
        wj = jnp.exp(last[:, None, :] - cum)
        s = s * jnp.exp(last)[:, :, None, None] + jnp.einsum('bjhp,bjhn->bhpn', xdt * wj[..., None], bh)
        return s, y

    s, y = lax.scan(step, ssm_state.astype(F32), tuple(to_chunks(a, qn) for a in (xs, bm, cm, dt, la)))
    y = from_chunks(y) + xs.astype(F32) * d_skip.astype(F32)[:, None]
    y = y.reshape(b, l, SSD_DINNER) * jax.nn.silu(z.astype(F32))
    yg = y.reshape(b, l, SSD_GROUPS, SSD_DINNER // SSD_GROUPS)
    yg = yg * lax.rsqrt(jnp.mean(yg * yg, -1, keepdims=True) + EPS)
    y = (yg.reshape(b, l, SSD_DINNER) * norm_w.astype(F32)).astype(h.dtype)
    return y @ w_out, s.astype(ssm_state.dtype), new_conv.astype(conv_state.dtype)


def mlstm_mixer(h, c_state, n_state, m_state, w_in, b_gates, w_out):
    b, l, _ = h.shape
    proj = h @ w_in
    q = proj[..., :ML_QK].reshape(b, l, ML_HEADS, ML_DQK)
    k = proj[..., ML_QK:2 * ML_QK].reshape(b, l, ML_HEADS, ML_DQK) * (ML_DQK ** -0.5)
    v = proj[..., 2 * ML_QK:2 * ML_QK + D_MODEL].reshape(b, l, ML_HEADS, ML_DV)
    og = proj[..., 2 * ML_QK + D_MODEL:2 * ML_QK + 2 * D_MODEL]
    gates = proj[..., 2 * ML_QK + 2 * D_MODEL:].astype(F32) + b_gates.astype(F32)
    ig = gates[..., :ML_HEADS]
    lf = jax.nn.log_sigmoid(gates[..., ML_HEADS:])
    qn = chunk_len(l)
    causal = jnp.tril(jnp.ones((qn, qn), bool))[None, :, :, None]

    def step(carry, inp):
        c, n, m = carry
        qc, kc, vc, igc, lfc = inp
        qc, kc, vc = qc.astype(F32), kc.astype(F32), vc.astype(F32)
        bcum = jnp.cumsum(lfc, axis=1)
        logw = jnp.where(causal, bcum[:, :, None, :] - bcum[:, None, :, :] + igc[:, None, :, :], -jnp.inf)
        logs = bcum + m[:, None, :]
        mt = jnp.maximum(logs, jnp.max(logw, axis=2))
        qk = jnp.einsum('bihd,bjhd->bijh', qc, kc) * jnp.exp(logw - mt[:, :, None, :])
        ws = jnp.exp(logs - mt)
        num = jnp.einsum('bijh,bjhe->bihe', qk, vc) + jnp.einsum('bihd,bhde->bihe', qc, c) * ws[..., None]
        den = jnp.sum(qk, axis=2) + jnp.einsum('bihd,bhd->bih', qc, n) * ws
        hc = num / jnp.maximum(jnp.abs(den), jnp.exp(-mt))[..., None]
        m_new = mt[:, -1]
        wl = jnp.exp(bcum[:, -1:, :] - bcum + igc - m_new[:, None, :])
        sl = jnp.exp(bcum[:, -1] + m - m_new)
        c = c * sl[..., None, None] + jnp.einsum('bjhd,bjhe->bhde', kc * wl[..., None], vc)
        n = n * sl[..., None] + jnp.einsum('bjhd,bjh->bhd', kc, wl)
        return (c, n, m_new), hc

    (c, n, m), hs = lax.scan(step, (c_state.astype(F32), n_state.astype(F32), m_state.astype(F32)),
                             tuple(to_chunks(a, qn) for a in (q, k, v, ig, lf)))
    hs = from_chunks(hs).reshape(b, l, D_MODEL)
    out = (jax.nn.sigmoid(og.astype(F32)) * hs).astype(h.dtype) @ w_out
    return out, c.astype(c_state.dtype), n.astype(n_state.dtype), m.astype(m_state.dtype)


def trunk(x, pos0, st_ret, st_ssm, st_conv, st_c, st_n, st_m, weights):
    (norm_mix_pre, norm_mix_post, norm_ffn_pre, norm_ffn_post,
     ret_w_in, ret_w_out,
     ssd_w_in, ssd_conv_w, ssd_conv_b, ssd_dt_bias, ssd_a_log, ssd_d, ssd_norm_w, ssd_w_out,
     ml_w_in, ml_b_gates, ml_w_out, ffn_w_up, ffn_w_down) = weights
    new_ret, new_ssm, new_conv, new_c, new_n, new_m = [], [], [], [], [], []
    for i in range(DEPTH):
        j = i // N_MIXERS
        hn = rmsnorm(x, norm_mix_pre[i])
        if i % N_MIXERS == 0:
            out, s = retention_mixer(hn, pos0, st_ret[j], ret_w_in[j], ret_w_out[j])
            new_ret.append(s)
        elif i % N_MIXERS == 1:
            out, s, cv = ssd_mixer(hn, st_ssm[j], st_conv[j], ssd_w_in[j], ssd_conv_w[j], ssd_conv_b[j],
                                   ssd_dt_bias[j], ssd_a_log[j], ssd_d[j], ssd_norm_w[j], ssd_w_out[j])
            new_ssm.append(s)
            new_conv.append(cv)
        else:
            out, c, n, m = mlstm_mixer(hn, st_c[j], st_n[j], st_m[j], ml_w_in[j], ml_b_gates[j], ml_w_out[j])
            new_c.append(c)
            new_n.append(n)
            new_m.append(m)
        x = x + rmsnorm(out, norm_mix_post[i])
        hn = rmsnorm(x, norm_ffn_pre[i])
        f = jnp.square(jax.nn.relu(hn @ ffn_w_up[i])) @ ffn_w_down[i]
        x = x + rmsnorm(f, norm_ffn_post[i])
    return x, (jnp.stack(new_ret), jnp.stack(new_ssm), jnp.stack(new_conv),
               jnp.stack(new_c), jnp.stack(new_n), jnp.stack(new_m))


def setup_inputs(seed: int = 0) -> dict:
    key = jax.random.key(seed)
    k = jax.random.split(key, 28)
    nrm = lambda kk, shape, scale: jax.random.normal(kk, shape, F32) * scale
    gain = lambda kk, shape: 1.0 + 0.05 * jax.random.normal(kk, shape, F32)
    dt = jnp.exp(jax.random.uniform(k[17], (N_SSD, SSD_HEADS), F32, math.log(1e-3), math.log(1e-1)))
    return {
        "x_prompt": nrm(k[0], (BATCH, SEQ, D_MODEL), 1.0),
        "x_sample": nrm(k[1], (DEC_BATCH, DEC_SEQ, D_MODEL), 1.0),
        "state_ret": nrm(k[2], (N_RET, DEC_BATCH, RET_HEADS, RET_DK, RET_DV), 1.0),
        "state_ssm": nrm(k[3], (N_SSD, DEC_BATCH, SSD_HEADS, SSD_HEADDIM, SSD_STATE), 0.1),
        "state_conv": nrm(k[4], (N_SSD, DEC_BATCH, SSD_CONV - 1, SSD_CONV_DIM), 1.0),
        "state_mlstm_C": nrm(k[5], (N_ML, DEC_BATCH, ML_HEADS, ML_DQK, ML_DV), 1.0),
        "state_mlstm_n": jnp.abs(nrm(k[6], (N_ML, DEC_BATCH, ML_HEADS, ML_DQK), 1.0)),
        "state_mlstm_m": nrm(k[7], (N_ML, DEC_BATCH, ML_HEADS), 0.5),
        "norm_mix_pre": gain(k[8], (DEPTH, D_MODEL)),
        "norm_mix_post": gain(k[9], (DEPTH, D_MODEL)),
        "norm_ffn_pre": gain(k[10], (DEPTH, D_MODEL)),
        "norm_ffn_post": gain(k[11], (DEPTH, D_MODEL)),
        "ret_w_in": nrm(k[12], (N_RET, D_MODEL, 4 * D_MODEL), D_MODEL ** -0.5),
        "ret_w_out": nrm(k[13], (N_RET, D_MODEL, D_MODEL), D_MODEL ** -0.5),
        "ssd_w_in": nrm(k[14], (N_SSD, D_MODEL, SSD_PROJ), D_MODEL ** -0.5),
        "ssd_conv_w": nrm(k[15], (N_SSD, SSD_CONV, SSD_CONV_DIM), SSD_CONV ** -0.5),
        "ssd_conv_b": nrm(k[16], (N_SSD, SSD_CONV_DIM), 0.02),
        "ssd_dt_bias": dt + jnp.log(-jnp.expm1(-dt)),
        "ssd_a_log": jnp.log(jax.random.uniform(k[18], (N_SSD, SSD_HEADS), F32, 1.0, 16.0)),
        "ssd_d": 1.0 + 0.1 * jax.random.normal(k[19], (N_SSD, SSD_HEADS), F32),
        "ssd_norm_w": gain(k[20], (N_SSD, SSD_DINNER)),
        "ssd_w_out": nrm(k[21], (N_SSD, SSD_DINNER, D_MODEL), SSD_DINNER ** -0.5),
        "ml_w_in": nrm(k[22], (N_ML, D_MODEL, ML_PROJ), D_MODEL ** -0.5),
        "ml_b_gates": jnp.concatenate([
            nrm(k[23], (N_ML, ML_HEADS), 0.1),
            jnp.linspace(3.0, 6.0, ML_HEADS, dtype=F32)[None, :] + nrm(k[24], (N_ML, ML_HEADS), 0.1)], axis=-1),
        "ml_w_out": nrm(k[25], (N_ML, D_MODEL, D_MODEL), D_MODEL ** -0.5),
        "ffn_w_up": nrm(k[26], (DEPTH, D_MODEL, D_FF), D_MODEL ** -0.5),
        "ffn_w_down": nrm(k[27], (DEPTH, D_FF, D_MODEL), D_FF ** -0.5),
    }


def reference(x_prompt, x_sample, state_ret, state_ssm, state_conv, state_mlstm_C, state_mlstm_n,
              state_mlstm_m, norm_mix_pre, norm_mix_post, norm_ffn_pre, norm_ffn_post,
              ret_w_in, ret_w_out, ssd_w_in, ssd_conv_w, ssd_conv_b, ssd_dt_bias, ssd_a_log, ssd_d,
              ssd_norm_w, ssd_w_out, ml_w_in, ml_b_gates, ml_w_out, ffn_w_up, ffn_w_down):
    weights = (norm_mix_pre, norm_mix_post, norm_ffn_pre, norm_ffn_post,
               ret_w_in, ret_w_out,
               ssd_w_in, ssd_conv_w, ssd_conv_b, ssd_dt_bias, ssd_a_log, ssd_d, ssd_norm_w, ssd_w_out,
               ml_w_in, ml_b_gates, ml_w_out, ffn_w_up, ffn_w_down)
    dtp = x_prompt.dtype
    y_prompt, (ret_p, ssm_p, conv_p, mc_p, mn_p, mm_p) = trunk(
        x_prompt, 0,
        jnp.zeros((N_RET, BATCH, RET_HEADS, RET_DK, RET_DV), dtp),
        jnp.zeros((N_SSD, BATCH, SSD_HEADS, SSD_HEADDIM, SSD_STATE), dtp),
        jnp.zeros((N_SSD, BATCH, SSD_CONV - 1, SSD_CONV_DIM), dtp),
        jnp.zeros((N_ML, BATCH, ML_HEADS, ML_DQK, ML_DV), dtp),
        jnp.zeros((N_ML, BATCH, ML_HEADS, ML_DQK), dtp),
        jnp.zeros((N_ML, BATCH, ML_HEADS), dtp),
        weights)
    y_sample, (ret_s, ssm_s, conv_s, mc_s, mn_s, mm_s) = trunk(
        x_sample, PAST_LEN, state_ret, state_ssm, state_conv,
        state_mlstm_C, state_mlstm_n, state_mlstm_m, weights)
    return (y_prompt, y_sample, ret_p, ret_s, ssm_p, ssm_s, conv_p, conv_s,
            mc_p, mc_s, mn_p, mn_s, mm_p, mm_s)
```

```python
import functools
import math

import jax
import jax.numpy as jnp
from jax import lax
from jax.experimental import pallas as pl
from jax.experimental.pallas import tpu as pltpu

F32 = jnp.float32
BF16 = jnp.bfloat16
HIGHEST = lax.Precision.HIGHEST

EPS = 1e-6
PAST_LEN = 16384
ROPE_BASE = 10000.0
RET_DK = 256
SSD_HEADDIM = 64
SSD_STATE = 128
SSD_REP = 16
SSD_CONV = 4
SSD_GROUP_W = SSD_REP * SSD_HEADDIM
ML_DQK = 256
ML_DV = 512

LANES = 128
SUBLANES = 8
VMEM_LIMIT_BYTES = 48 * 1024 * 1024


def _params(sem):
    return pltpu.CompilerParams(dimension_semantics=sem, vmem_limit_bytes=VMEM_LIMIT_BYTES)


def _tile(dim, pref, mult):
    best = None
    t = mult
    while t <= min(dim, pref):
        if dim % t == 0:
            best = t
        t += mult
    assert best is not None, (dim, pref, mult)
    return best


def _nt(a, b):
    return lax.dot_general(a, b, (((1,), (1,)), ((), ())), preferred_element_type=F32)


def _tn(a, b):
    return lax.dot_general(a, b, (((0,), (0,)), ((), ())), preferred_element_type=F32)


def _silu(x):
    return x * jax.nn.sigmoid(x)


def _softplus(x):
    return jnp.maximum(x, 0.0) + jnp.log1p(jnp.exp(-jnp.abs(x)))


def _log_sigmoid(x):
    return -_softplus(-x)


def _prenorm_kernel(x_ref, w_ref, h_ref):
    x = x_ref[...]
    h_ref[...] = (x * lax.rsqrt(jnp.mean(x * x, -1, keepdims=True) + EPS) * w_ref[...]).astype(h_ref.dtype)


def _prenorm(x, w):
    m, d = x.shape
    tm = _tile(m, 512, SUBLANES)
    return pl.pallas_call(
        _prenorm_kernel,
        grid=(m // tm,),
        in_specs=[pl.BlockSpec((tm, d), lambda i: (i, 0)), pl.BlockSpec((1, d), lambda i: (0, 0))],
        out_specs=pl.BlockSpec((tm, d), lambda i: (i, 0)),
        out_shape=jax.ShapeDtypeStruct((m, d), BF16),
        compiler_params=_params(("parallel",)),
        name="prenorm",
    )(x, w.reshape(1, d))


def _post_kernel(o_ref, x_ref, wpost_ref, wpre_ref, xn_ref, hn_ref):
    o = o_ref[...]
    y = o * lax.rsqrt(jnp.mean(o * o, -1, keepdims=True) + EPS) * wpost_ref[...]
    xn = x_ref[...] + y
    xn_ref[...] = xn
    hn_ref[...] = (xn * lax.rsqrt(jnp.mean(xn * xn, -1, keepdims=True) + EPS) * wpre_ref[...]).astype(hn_ref.dtype)


def _post_residual_prenorm(o, x, w_post, w_pre_next):
    m, d = x.shape
    tm = _tile(m, 256, SUBLANES)
    row = pl.BlockSpec((tm, d), lambda i: (i, 0))
    vec = pl.BlockSpec((1, d), lambda i: (0, 0))
    return pl.pallas_call(
        _post_kernel,
        grid=(m // tm,),
        in_specs=[row, row, vec, vec],
        out_specs=[row, row],
        out_shape=[jax.ShapeDtypeStruct((m, d), F32), jax.ShapeDtypeStruct((m, d), BF16)],
        compiler_params=_params(("parallel",)),
        name="post_residual_prenorm",
    )(o, x, w_post.reshape(1, d), w_pre_next.reshape(1, d))


def _mm_kernel(a_ref, w_ref, o_ref, *scratch, nk, act):
    def finish(r):
        if act == "relu2":
            r = jnp.square(jnp.maximum(r, 0.0))
        o_ref[...] = r.astype(o_ref.dtype)

    if nk == 1:
        finish(jnp.dot(a_ref[...], w_ref[...], preferred_element_type=F32))
        return
    acc_ref, = scratch
    k = pl.program_id(2)

    @pl.when(k == 0)
    def _():
        acc_ref[...] = jnp.zeros_like(acc_ref)

    acc_ref[...] += jnp.dot(a_ref[...], w_ref[...], preferred_element_type=F32)

    @pl.when(k == nk - 1)
    def _():
        finish(acc_ref[...])


def _matmul(a, w, n_cols=None, col_off=0, act=None, out_dtype=F32):
    m, kdim = a.shape
    n_cols = w.shape[1] if n_cols is None else n_cols
    tm = _tile(m, 1024, SUBLANES)
    tn = _tile(math.gcd(n_cols, col_off) if col_off else n_cols, 1024, LANES)
    tk = _tile(kdim, 1024, LANES)
    nk = kdim // tk
    joff = col_off // tn
    return pl.pallas_call(
        functools.partial(_mm_kernel, nk=nk, act=act),
        grid=(m // tm, n_cols // tn, nk),
        in_specs=[pl.BlockSpec((tm, tk), lambda i, j, k: (i, k)),
                  pl.BlockSpec((tk, tn), lambda i, j, k: (k, j + joff))],
        out_specs=pl.BlockSpec((tm, tn), lambda i, j, k: (i, j)),
        out_shape=jax.ShapeDtypeStruct((m, n_cols), out_dtype),
        scratch_shapes=[pltpu.VMEM((tm, tn), F32)] if nk > 1 else [],
        compiler_params=_params(("parallel", "parallel", "arbitrary")),
        name="matmul" + ("_" + act if act else ""),
    )(a, w)


def _ret_kernel(q_ref, k_ref, v_ref, g_ref, cos_ref, sin_ref, intra_ref, cross_ref, into_ref, whole_ref,
                s0_ref, o_ref, sout_ref, s_scr, *, nc):
    c = pl.program_id(2)

    @pl.when(c == 0)
    def _():
        s_scr[...] = s0_ref[0, 0]

    cos = cos_ref[...]
    sin = sin_ref[...]
    half = RET_DK // 2

    def rot(x):
        x1 = x[:, :half]
        x2 = x[:, half:]
        return jnp.concatenate([x1 * cos - x2 * sin, x2 * cos + x1 * sin], axis=1)

    qr = rot(q_ref[...]).astype(BF16)
    kr = rot(k_ref[...]) * (RET_DK ** -0.5)
    v = v_ref[...].astype(BF16)
    s = s_scr[...]
    sc = _nt(qr, kr.astype(BF16)) * intra_ref[0]
    o = (jnp.dot(sc.astype(BF16), v, preferred_element_type=F32)
         + jnp.dot(qr, s.astype(BF16), preferred_element_type=F32) * cross_ref[0])
    s_new = s * whole_ref[0] + _tn((kr * into_ref[0]).astype(BF16), v)
    s_scr[...] = s_new
    mu = jnp.mean(o, -1, keepdims=True)
    oc = o - mu
    var = jnp.mean(oc * oc, -1, keepdims=True)
    o_ref[...] = (_silu(g_ref[...]) * (oc * lax.rsqrt(var + EPS))).astype(o_ref.dtype)

    @pl.when(c == nc - 1)
    def _():
        sout_ref[0, 0] = s_new


def _retention(proj, row_off, b, l, q, pos0, state, out_dtype):
    d = proj.shape[1] // 4
    h = d // RET_DK
    nc = l // q
    rb0 = row_off // q
    assert row_off % q == 0 and l % q == 0
    half = RET_DK // 2
    inv = jnp.power(ROPE_BASE, -jnp.arange(half, dtype=F32) / half)
    ang = (pos0 + jnp.arange(l, dtype=F32))[:, None] * inv[None, :]
    cos, sin = jnp.cos(ang), jnp.sin(ang)
    idx = jnp.arange(q, dtype=F32)
    diff = idx[:, None] - idx[None, :]
    logg = jnp.log1p(-jnp.exp2(-5.0 - jnp.arange(h, dtype=F32)))
    intra = jnp.where(diff[None] >= 0, jnp.exp(logg[:, None, None] * jnp.maximum(diff, 0.0)[None]), 0.0)
    cross = jnp.exp(logg[:, None] * (idx[None, :] + 1.0))[:, :, None]
    into = jnp.exp(logg[:, None] * (q - 1.0 - idx[None, :]))[:, :, None]
    whole = jnp.exp(logg * q)[:, None, None]

    def part(p):
        return pl.BlockSpec((q, RET_DK), lambda bi, hi, ci: (rb0 + bi * nc + ci, p * h + hi))

    tab = pl.BlockSpec((q, half), lambda bi, hi, ci: (ci, 0))
    col = pl.BlockSpec((1, q, 1), lambda bi, hi, ci: (hi, 0, 0))
    st = pl.BlockSpec((1, 1, RET_DK, RET_DK), lambda bi, hi, ci: (bi, hi, 0, 0))
    return pl.pallas_call(
        functools.partial(_ret_kernel, nc=nc),
        grid=(b, h, nc),
        in_specs=[part(0), part(1), part(2), part(3), tab, tab,
                  pl.BlockSpec((1, q, q), lambda bi, hi, ci: (hi, 0, 0)), col, col,
                  pl.BlockSpec((1, 1, 1), lambda bi, hi, ci: (hi, 0, 0)), st],
        out_specs=[pl.BlockSpec((q, RET_DK), lambda bi, hi, ci: (bi * nc + ci, hi)), st],
        out_shape=[jax.ShapeDtypeStruct((b * l, d), out_dtype),
                   jax.ShapeDtypeStruct((b, h, RET_DK, RET_DK), F32)],
        scratch_shapes=[pltpu.VMEM((RET_DK, RET_DK), F32)],
        compiler_params=_params(("parallel", "parallel", "arbitrary")),
        name="retention",
    )(proj, proj, proj, proj, cos, sin, intra, cross, into, whole, state)


def _conv_kernel(cur_ref, prev_ref, st_ref, w_ref, b_ref, o_ref, *, blocks_per_seq):
    r = pl.program_id(0)
    cur = cur_ref[...]
    first = (r % blocks_per_seq) == 0
    prev = jnp.where(first, st_ref[0], prev_ref[...])
    w = w_ref[...]
    row8 = lax.broadcasted_iota(jnp.int32, prev.shape, 0)
    acc = b_ref[...] + cur * w[SSD_CONV - 1:SSD_CONV, :]
    for tap in range(SSD_CONV - 1):
        sh = SSD_CONV - 1 - tap
        rolled = pltpu.roll(cur, sh, 0)
        head = jnp.where(row8 < sh, pltpu.roll(prev, sh, 0), rolled[:SUBLANES])
        if cur.shape[0] > SUBLANES:
            shifted = jnp.concatenate([head, rolled[SUBLANES:]], axis=0)
        else:
            shifted = head
        acc = acc + shifted * w[tap:tap + 1, :]
    o_ref[...] = _silu(acc)


def _ssd_conv(proj, row_off, b, l, col0, conv_dim, state8, conv_w, conv_b):
    t = _tile(l, 256, SUBLANES)
    tc = _tile(math.gcd(col0, conv_dim), 2048, LANES)
    bps = l // t
    rb0 = row_off // t
    rb8 = row_off // SUBLANES
    t8 = t // SUBLANES
    j0 = col0 // tc
    return pl.pallas_call(
        functools.partial(_conv_kernel, blocks_per_seq=bps),
        grid=(b * bps, conv_dim // tc),
        in_specs=[pl.BlockSpec((t, tc), lambda r, j: (rb0 + r, j0 + j)),
                  pl.BlockSpec((SUBLANES, tc), lambda r, j: (jnp.maximum(rb8 + r * t8 - 1, 0), j0 + j)),
                  pl.BlockSpec((1, SUBLANES, tc), lambda r, j: (r // bps, 0, j)),
                  pl.BlockSpec((SSD_CONV, tc), lambda r, j: (0, j)),
                  pl.BlockSpec((1, tc), lambda r, j: (0, j))],
        out_specs=pl.BlockSpec((t, tc), lambda r, j: (r, j)),
        out_shape=jax.ShapeDtypeStruct((b * l, conv_dim), F32),
        compiler_params=_params(("parallel", "parallel")),
        name="ssd_conv",
    )(proj, proj, state8, conv_w, conv_b.reshape(1, conv_dim))


def _ssd_kernel(xs_ref, bm_ref, cm_ref, z_ref, dtc_ref, dtr_ref, biasc_ref, biasr_ref, alogc_ref, alogr_ref,
                dexp_ref, nw_ref, e_ref, s0_ref, y_ref, sout_ref, s_scr, yi_scr, *, nc, q):
    c = pl.program_id(2)

    @pl.when(c == 0)
    def _():
        s_scr[...] = s0_ref[0, 0]

    ii = lax.broadcasted_iota(jnp.int32, (q, q), 0)
    jj = lax.broadcasted_iota(jnp.int32, (q, q), 1)
    causal = ii >= jj
    tri = causal.astype(F32)
    triu = (jj >= ii).astype(F32)

    dtc = _softplus(dtc_ref[0, 0] + biasc_ref[0])
    lac = dtc * (-jnp.exp(alogc_ref[0]))
    dtr = _softplus(dtr_ref[0, 0] + biasr_ref[0])
    lar = dtr * (-jnp.exp(alogr_ref[0]))
    cum_c = jnp.dot(tri, lac, preferred_element_type=F32, precision=HIGHEST)
    cum_r = jnp.dot(lar, triu, preferred_element_type=F32, precision=HIGHEST)
    last = cum_c[q - 1:q, :]

    e = e_ref[...]

    def expand(a):
        return jnp.dot(a, e, preferred_element_type=F32, precision=HIGHEST)

    xs = xs_ref[...]
    xdt = xs * expand(dtc)
    xdt_b = xdt.astype(BF16)
    bm = bm_ref[...].astype(BF16)
    cm = cm_ref[...].astype(BF16)
    cb = _nt(cm, bm)
    lane = lax.broadcasted_iota(jnp.int32, (q, LANES), 1)
    for p in range(SSD_REP // 2):
        x2 = xdt_b[:, p * LANES:(p + 1) * LANES]
        ys = []
        for hh in (2 * p, 2 * p + 1):
            seg = cum_c[:, hh:hh + 1] - cum_r[hh:hh + 1, :]
            dec = jnp.where(causal, jnp.exp(jnp.where(causal, seg, 0.0)), 0.0)
            ys.append(jnp.dot((cb * dec).astype(BF16), x2, preferred_element_type=F32))
        yi_scr[:, p * LANES:(p + 1) * LANES] = jnp.where(lane < SSD_HEADDIM, ys[0], ys[1])

    s = s_scr[...]
    y = yi_scr[...] + _nt(cm, s.astype(BF16)) * expand(jnp.exp(cum_c))
    wj = jnp.exp(last - cum_c)
    upd = _tn((xdt * expand(wj)).astype(BF16), bm)
    elast = jnp.exp(last)
    for hh in range(SSD_REP):
        rows = slice(hh * SSD_HEADDIM, (hh + 1) * SSD_HEADDIM)
        s_scr[rows, :] = s[rows, :] * elast[:, hh:hh + 1] + upd[rows, :]

    y = y + xs * dexp_ref[0]
    y = y * _silu(z_ref[...])
    y = y * lax.rsqrt(jnp.mean(y * y, -1, keepdims=True) + EPS)
    y_ref[...] = (y * nw_ref[0]).astype(y_ref.dtype)

    @pl.when(c == nc - 1)
    def _():
        sout_ref[0, 0] = s_scr[...]


def _ssd_scan(proj, xact, dt_raw, row_off, b, l, q, state, dt_bias, a_log, d_skip, norm_w, out_dtype):
    conv_dim = xact.shape[1]
    heads = dt_raw.shape[1]
    g = heads // SSD_REP
    dinner = heads * SSD_HEADDIM
    gn = g * SSD_STATE
    assert conv_dim == dinner + 2 * gn and l % q == 0 and row_off % q == 0
    nc = l // q
    nblk = b * nc
    rb0 = row_off // q
    dtc = dt_raw.reshape(nblk, q, g, SSD_REP).transpose(2, 0, 1, 3)
    dtr = dt_raw.reshape(nblk, q, g, SSD_REP).transpose(2, 0, 3, 1)
    vec_c = lambda a: a.reshape(g, 1, SSD_REP)
    vec_r = lambda a: a.reshape(g, SSD_REP, 1)
    dexp = jnp.repeat(d_skip, SSD_HEADDIM).reshape(g, 1, SSD_GROUP_W)
    nw = norm_w.reshape(g, 1, SSD_GROUP_W)
    e = (jnp.arange(SSD_GROUP_W)[None, :] // SSD_HEADDIM == jnp.arange(SSD_REP)[:, None]).astype(F32)
    st = state.reshape(b, g, SSD_GROUP_W, SSD_STATE)
    nb0 = dinner // SSD_STATE

    blk = lambda bi, gi, ci: bi * nc + ci
    vc = pl.BlockSpec((1, 1, SSD_REP), lambda bi, gi, ci: (gi, 0, 0))
    vr = pl.BlockSpec((1, SSD_REP, 1), lambda bi, gi, ci: (gi, 0, 0))
    vg = pl.BlockSpec((1, 1, SSD_GROUP_W), lambda bi, gi, ci: (gi, 0, 0))
    stspec = pl.BlockSpec((1, 1, SSD_GROUP_W, SSD_STATE), lambda bi, gi, ci: (bi, gi, 0, 0))
    y, s_out = pl.pallas_call(
        functools.partial(_ssd_kernel, nc=nc, q=q),
        grid=(b, g, nc),
        in_specs=[pl.BlockSpec((q, SSD_GROUP_W), lambda bi, gi, ci: (blk(bi, gi, ci), gi)),
                  pl.BlockSpec((q, SSD_STATE), lambda bi, gi, ci: (blk(bi, gi, ci), nb0 + gi)),
                  pl.BlockSpec((q, SSD_STATE), lambda bi, gi, ci: (blk(bi, gi, ci), nb0 + g + gi)),
                  pl.BlockSpec((q, SSD_GROUP_W), lambda bi, gi, ci: (rb0 + blk(bi, gi, ci), gi)),
                  pl.BlockSpec((1, 1, q, SSD_REP), lambda bi, gi, ci: (gi, blk(bi, gi, ci), 0, 0)),
                  pl.BlockSpec((1, 1, SSD_REP, q), lambda bi, gi, ci: (gi, blk(bi, gi, ci), 0, 0)),
                  vc, vr, vc, vr, vg, vg,
                  pl.BlockSpec((SSD_REP, SSD_GROUP_W), lambda bi, gi, ci: (0, 0)),
                  stspec],
        out_specs=[pl.BlockSpec((q, SSD_GROUP_W), lambda bi, gi, ci: (blk(bi, gi, ci), gi)), stspec],
        out_shape=[jax.ShapeDtypeStruct((b * l, dinner), out_dtype),
                   jax.ShapeDtypeStruct(st.shape, F32)],
        scratch_shapes=[pltpu.VMEM((SSD_GROUP_W, SSD_STATE), F32), pltpu.VMEM((q, SSD_GROUP_W), F32)],
        compiler_params=_params(("parallel", "parallel", "arbitrary")),
        name="ssd_scan",
    )(xact, xact, xact, proj, dtc, dtr, vec_c(dt_bias), vec_r(dt_bias), vec_c(a_log), vec_r(a_log),
      dexp, nw, e, st)
    return y, s_out.reshape(state.shape)


def _mlstm_kernel(q_ref, k_ref, v_ref, og_ref, gc_ref, gr_ref, bc_ref, br_ref, c0_ref, n0_ref, m0_ref,
                  h_ref, cout_ref, nout_ref, mout_ref, c_scr, n_scr, m_scr, *, nc, q):
    c = pl.program_id(2)

    @pl.when(c == 0)
    def _():
        c_scr[...] = c0_ref[0, 0]
        n_scr[...] = n0_ref[0, 0]
        m_scr[...] = m0_ref[0, 0]

    ii = lax.broadcasted_iota(jnp.int32, (q, q), 0)
    jj = lax.broadcasted_iota(jnp.int32, (q, q), 1)
    causal = ii >= jj
    tri = causal.astype(F32)
    triu = (jj >= ii).astype(F32)

    gcol = gc_ref[0, 0] + bc_ref[0]
    grow = gr_ref[0, 0] + br_ref[0]
    ig_c = gcol[:, 0:1]
    ig_r = grow[0:1, :]
    lf_c = _log_sigmoid(gcol)
    lf_r = _log_sigmoid(grow)
    bcum_c = jnp.dot(tri, lf_c, preferred_element_type=F32, precision=HIGHEST)[:, 1:2]
    bcum_r = jnp.dot(lf_r, triu, preferred_element_type=F32, precision=HIGHEST)[1:2, :]

    m_prev = m_scr[...]
    logw = jnp.where(causal, bcum_c - bcum_r + ig_r, -jnp.inf)
    logs = bcum_c + m_prev
    mt = jnp.maximum(logs, jnp.max(logw, axis=1, keepdims=True))
    qf = q_ref[...]
    kf = k_ref[...] * (ML_DQK ** -0.5)
    qb = qf.astype(BF16)
    v = v_ref[...].astype(BF16)
    qk = _nt(qb, kf.astype(BF16)) * jnp.exp(logw - mt)
    ws = jnp.exp(logs - mt)
    cs = c_scr[...]
    ns = n_scr[...]
    num = (jnp.dot(qk.astype(BF16), v, preferred_element_type=F32)
           + jnp.dot(qb, cs.astype(BF16), preferred_element_type=F32) * ws)
    den = jnp.sum(qk, axis=1, keepdims=True) + jnp.sum(qf * ns, axis=1, keepdims=True) * ws
    hc = num / jnp.maximum(jnp.abs(den), jnp.exp(-mt))
    h_ref[...] = (jax.nn.sigmoid(og_ref[...]) * hc).astype(h_ref.dtype)

    m_new = mt[q - 1:q, :]
    blast = bcum_c[q - 1:q, :]
    wl = jnp.exp(blast - bcum_c + ig_c - m_new)
    sl = jnp.exp(blast + m_prev - m_new)
    kw = kf * wl
    c_new = cs * sl + _tn(kw.astype(BF16), v)
    n_new = ns * sl + jnp.sum(kw, axis=0, keepdims=True)
    c_scr[...] = c_new
    n_scr[...] = n_new
    m_scr[...] = m_new

    @pl.when(c == nc - 1)
    def _():
        cout_ref[0, 0] = c_new
        nout_ref[0, 0] = n_new
        mout_ref[0, 0] = m_new


def _mlstm(proj, graw, row_off, b, l, q, c_state, n_state, m_state, b_gates, out_dtype):
    h = graw.shape[1] // 2
    d = h * ML_DV
    nc = l // q
    nblk = b * nc
    rb0 = row_off // q
    assert row_off % q == 0 and l % q == 0
    g4 = graw.reshape(nblk, q, 2, h)
    gc = g4.transpose(3, 0, 1, 2)
    gr = g4.transpose(3, 0, 2, 1)
    bc = b_gates.reshape(2, h).T.reshape(h, 1, 2)
    br = b_gates.reshape(2, h).T.reshape(h, 2, 1)
    kb0 = h
    vb0 = (2 * h * ML_DQK) // ML_DV
    ob0 = vb0 + h

    blk = lambda bi, hi, ci: bi * nc + ci
    cst = pl.BlockSpec((1, 1, ML_DQK, ML_DV), lambda bi, hi, ci: (bi, hi, 0, 0))
    nst = pl.BlockSpec((1, 1, 1, ML_DQK), lambda bi, hi, ci: (bi, hi, 0, 0))
    mst = pl.BlockSpec((1, 1, 1, 1), lambda bi, hi, ci: (bi, hi, 0, 0))
    hs, c_out, n_out, m_out = pl.pallas_call(
        functools.partial(_mlstm_kernel, nc=nc, q=q),
        grid=(b, h, nc),
        in_specs=[pl.BlockSpec((q, ML_DQK), lambda bi, hi, ci: (rb0 + blk(bi, hi, ci), hi)),
                  pl.BlockSpec((q, ML_DQK), lambda bi, hi, ci: (rb0 + blk(bi, hi, ci), kb0 + hi)),
                  pl.BlockSpec((q, ML_DV), lambda bi, hi, ci: (rb0 + blk(bi, hi, ci), vb0 + hi)),
                  pl.BlockSpec((q, ML_DV), lambda bi, hi, ci: (rb0 + blk(bi, hi, ci), ob0 + hi)),
                  pl.BlockSpec((1, 1, q, 2), lambda bi, hi, ci: (hi, blk(bi, hi, ci), 0, 0)),
                  pl.BlockSpec((1, 1, 2, q), lambda bi, hi, ci: (hi, blk(bi, hi, ci), 0, 0)),
                  pl.BlockSpec((1, 1, 2), lambda bi, hi, ci: (hi, 0, 0)),
                  pl.BlockSpec((1, 2, 1), lambda bi, hi, ci: (hi, 0, 0)),
                  cst, nst, mst],
        out_specs=[pl.BlockSpec((q, ML_DV), lambda bi, hi, ci: (blk(bi, hi, ci), hi)), cst, nst, mst],
        out_shape=[jax.ShapeDtypeStruct((b * l, d), out_dtype),
                   jax.ShapeDtypeStruct((b, h, ML_DQK, ML_DV), F32),
                   jax.ShapeDtypeStruct((b, h, 1, ML_DQK), F32),
                   jax.ShapeDtypeStruct((b, h, 1, 1), F32)],
        scratch_shapes=[pltpu.VMEM((ML_DQK, ML_DV), F32), pltpu.VMEM((1, ML_DQK), F32), pltpu.VMEM((1, 1), F32)],
        compiler_params=_params(("parallel", "parallel", "arbitrary")),
        name="mlstm",
    )(proj, proj, proj, proj, gc, gr, bc, br, c_state, n_state.reshape(b, h, 1, ML_DQK),
      m_state.reshape(b, h, 1, 1))
    return hs, c_out, n_out.reshape(b, h, ML_DQK), m_out.reshape(b, h)


def _chunk(l, pref):
    return _tile(l, pref, SUBLANES)


def _pad_cols(w):
    return jnp.pad(w, ((0, 0), (0, -w.shape[1] % LANES)))


def kernel(x_prompt, x_sample, state_ret, state_ssm, state_conv, state_mlstm_C, state_mlstm_n, state_mlstm_m, norm_mix_pre, norm_mix_post, norm_ffn_pre, norm_ffn_post, ret_w_in, ret_w_out, ssd_w_in, ssd_conv_w, ssd_conv_b, ssd_dt_bias, ssd_a_log, ssd_d, ssd_norm_w, ssd_w_out, ml_w_in, ml_b_gates, ml_w_out, ffn_w_up, ffn_w_down):
    bp, lp, d = x_prompt.shape
    bs, ls, _ = x_sample.shape
    depth = norm_mix_pre.shape[0]
    mp, ms = bp * lp, bs * ls
    assert lp >= SSD_CONV - 1 and ls >= SSD_CONV - 1
    dinner = ssd_w_out.shape[1]
    ssd_heads = dinner // SSD_HEADDIM
    conv_dim = ssd_conv_w.shape[2]
    ml_heads = d // ML_DV
    ml_qk = ml_heads * ML_DQK
    ret_heads = d // RET_DK
    qs = _chunk(ls, 256)
    dt = x_prompt.dtype

    x = jnp.concatenate([x_prompt.reshape(mp, d), x_sample.reshape(ms, d)], axis=0)
    hn = _prenorm(x, norm_mix_pre[0])
    new = {k: ([], []) for k in ("ret", "ssm", "conv", "c", "n", "m")}

    for i in range(depth):
        j = i // 3
        kind = i % 3
        if kind == 0:
            proj = _matmul(hn, ret_w_in[j].astype(BF16))
            zero = jnp.zeros((bp, ret_heads, RET_DK, RET_DK), F32)
            op, sp = _retention(proj, 0, bp, lp, _chunk(lp, 256), 0.0, zero, BF16)
            os_, ss = _retention(proj, mp, bs, ls, qs, float(PAST_LEN), state_ret[j], F32)
            new["ret"][0].append(sp)
            new["ret"][1].append(ss)
            w_out = ret_w_out[j]
        elif kind == 1:
            w_in = ssd_w_in[j].astype(BF16)
            main = dinner + conv_dim
            proj = _matmul(hn, w_in, n_cols=main)
            dt_raw = _matmul(hn, _pad_cols(w_in[:, main:]))[:, :ssd_heads]
            outs = []
            for g_i, (off, b, l, q, st_ssm, st_conv) in enumerate((
                    (0, bp, lp, _chunk(lp, 128), jnp.zeros((bp,) + state_ssm.shape[2:], F32),
                     jnp.zeros((bp, SSD_CONV - 1, conv_dim), F32)),
                    (mp, bs, ls, qs, state_ssm[j], state_conv[j]))):
                st8 = jnp.pad(st_conv, ((0, 0), (SUBLANES - (SSD_CONV - 1), 0), (0, 0)))
                xact = _ssd_conv(proj, off, b, l, dinner, conv_dim, st8, ssd_conv_w[j], ssd_conv_b[j])
                y, s_new = _ssd_scan(proj, xact, dt_raw[off:off + b * l], off, b, l, q, st_ssm,
                                     ssd_dt_bias[j], ssd_a_log[j], ssd_d[j], ssd_norm_w[j],
                                     BF16 if g_i == 0 else F32)
                outs.append(y)
                new["ssm"][g_i].append(s_new)
                xbc = proj[off:off + b * l, dinner:].reshape(b, l, conv_dim)
                new["conv"][g_i].append(xbc[:, l - (SSD_CONV - 1):])
            op, os_ = outs
            w_out = ssd_w_out[j]
        else:
            w_in = ml_w_in[j].astype(BF16)
            main = 2 * ml_qk + 2 * d
            proj = _matmul(hn, w_in, n_cols=main)
            graw = _matmul(hn, _pad_cols(w_in[:, main:]))[:, :2 * ml_heads]
            op, cp, np_, mp_ = _mlstm(proj, graw[:mp], 0, bp, lp, _chunk(lp, 128),
                                      jnp.zeros((bp,) + state_mlstm_C.shape[2:], F32),
                                      jnp.zeros((bp,) + state_mlstm_n.shape[2:], F32),
                                      jnp.zeros((bp,) + state_mlstm_m.shape[2:], F32), ml_b_gates[j], BF16)
            os_, cs_, ns_, ms_ = _mlstm(proj, graw[mp:], mp, bs, ls, qs, state_mlstm_C[j], state_mlstm_n[j],
                                        state_mlstm_m[j], ml_b_gates[j], F32)
            for key, vp, vs in (("c", cp, cs_), ("n", np_, ns_), ("m", mp_, ms_)):
                new[key][0].append(vp)
                new[key][1].append(vs)
            w_out = ml_w_out[j]

        mixed = jnp.concatenate([op, os_.astype(BF16)], axis=0)
        out = _matmul(mixed, w_out.astype(BF16))
        x, hn = _post_residual_prenorm(out, x, norm_mix_post[i], norm_ffn_pre[i])
        up = _matmul(hn, ffn_w_up[i].astype(BF16), act="relu2", out_dtype=BF16)
        f = _matmul(up, ffn_w_down[i].astype(BF16))
        nxt = norm_mix_pre[i + 1] if i + 1 < depth else norm_mix_pre[i]
        x, hn = _post_residual_prenorm(f, x, norm_ffn_post[i], nxt)

    y_prompt = x[:mp].reshape(bp, lp, d)
    y_sample = x[mp:].reshape(bs, ls, d)
    st = lambda key, g_i: jnp.stack(new[key][g_i]).astype(dt)
    return (y_prompt, y_sample, st("ret", 0), st("ret", 1), st("ssm", 0), st("ssm", 1),
            st("conv", 0), st("conv", 1), st("c", 0), st("c", 1), st("n", 0), st("n", 1),
            st("m", 0), st("m", 1))
```

```python
import functools
import math

import jax
import jax.numpy as jnp
from jax import lax
from jax.experimental import pallas as pl
from jax.experimental.pallas import tpu as pltpu

F32 = jnp.float32
BF16 = jnp.bfloat16
HIGHEST = lax.Precision.HIGHEST

EPS = 1e-6
PAST_LEN = 16384
ROPE_BASE = 10000.0
RET_DK = 256
SSD_HEADDIM = 64
SSD_STATE = 128
SSD_REP = 16
SSD_CONV = 4
SSD_GROUP_W = SSD_REP * SSD_HEADDIM
ML_DQK = 256
ML_DV = 512

LANES = 128
SUBLANES = 8
BF16_ROWS = 16
VMEM_LIMIT_BYTES = 52 * 1024 * 1024
CAST_BLOCK_BYTES = 8 * 1024 * 1024


def _params(sem):
    return pltpu.CompilerParams(dimension_semantics=sem, vmem_limit_bytes=VMEM_LIMIT_BYTES)


def _tile(dim, pref, mult):
    best = None
    t = mult
    while t <= min(dim, pref):
        if dim % t == 0:
            best = t
        t += mult
    assert best is not None, (dim, pref, mult)
    return best


def _nt(a, b):
    return lax.dot_general(a, b, (((1,), (1,)), ((), ())), preferred_element_type=F32)


def _tn(a, b):
    return lax.dot_general(a, b, (((0,), (0,)), ((), ())), preferred_element_type=F32)


def _silu(x):
    return x * jax.nn.sigmoid(x)


def _softplus(x):
    return jnp.maximum(x, 0.0) + jnp.log1p(jnp.exp(-jnp.abs(x)))


def _log_sigmoid(x):
    return -_softplus(-x)


def _cast_kernel(w_ref, o_ref):
    o_ref[...] = w_ref[...].astype(o_ref.dtype)


def _to_bf16(w_stack, j):
    _, r, c = w_stack.shape
    tr = _tile(r, max(BF16_ROWS, CAST_BLOCK_BYTES // (4 * c)), BF16_ROWS)
    return pl.pallas_call(
        _cast_kernel,
        grid=(r // tr,),
        in_specs=[pl.BlockSpec((None, tr, c), lambda i: (j, i, 0))],
        out_specs=pl.BlockSpec((tr, c), lambda i: (i, 0)),
        out_shape=jax.ShapeDtypeStruct((r, c), BF16),
        compiler_params=_params(("parallel",)),
        name="cast_bf16",
    )(w_stack)


def _prenorm_kernel(x_ref, w_ref, h_ref):
    x = x_ref[...]
    h_ref[...] = (x * lax.rsqrt(jnp.mean(x * x, -1, keepdims=True) + EPS) * w_ref[...]).astype(h_ref.dtype)


def _prenorm(x, w):
    m, d = x.shape
    tm = _tile(m, 512, SUBLANES)
    return pl.pallas_call(
        _prenorm_kernel,
        grid=(m // tm,),
        in_specs=[pl.BlockSpec((tm, d), lambda i: (i, 0)), pl.BlockSpec((1, d), lambda i: (0, 0))],
        out_specs=pl.BlockSpec((tm, d), lambda i: (i, 0)),
        out_shape=jax.ShapeDtypeStruct((m, d), BF16),
        compiler_params=_params(("parallel",)),
        name="prenorm",
    )(x, w.reshape(1, d))


def _post_kernel(o_ref, x_ref, wpost_ref, *rest, with_next):
    o = o_ref[...]
    y = o * lax.rsqrt(jnp.mean(o * o, -1, keepdims=True) + EPS) * wpost_ref[...]
    xn = x_ref[...] + y
    if with_next:
        wpre_ref, xn_ref, hn_ref = rest
        hn_ref[...] = (xn * lax.rsqrt(jnp.mean(xn * xn, -1, keepdims=True) + EPS) * wpre_ref[...]).astype(hn_ref.dtype)
    else:
        xn_ref, = rest
    xn_ref[...] = xn


def _post_residual_prenorm(o, x, w_post, w_pre_next):
    m, d = x.shape
    tm = _tile(m, 256, SUBLANES)
    row = pl.BlockSpec((tm, d), lambda i: (i, 0))
    vec = pl.BlockSpec((1, d), lambda i: (0, 0))
    with_next = w_pre_next is not None
    xs = jax.ShapeDtypeStruct((m, d), F32)
    res = pl.pallas_call(
        functools.partial(_post_kernel, with_next=with_next),
        grid=(m // tm,),
        in_specs=[row, row, vec] + ([vec] if with_next else []),
        out_specs=[row, row] if with_next else [row],
        out_shape=[xs, jax.ShapeDtypeStruct((m, d), BF16)] if with_next else [xs],
        compiler_params=_params(("parallel",)),
        name="post_residual_prenorm",
    )(o, x, w_post.reshape(1, d), *([w_pre_next.reshape(1, d)] if with_next else []))
    return (res[0], res[1]) if with_next else (res[0], None)


def _mm_kernel(a_ref, w_ref, o_ref, *scratch, nk, act):
    def finish(r):
        if act == "relu2":
            r = jnp.square(jnp.maximum(r, 0.0))
        o_ref[...] = r.astype(o_ref.dtype)

    if nk == 1:
        finish(jnp.dot(a_ref[...], w_ref[...], preferred_element_type=F32))
        return
    acc_ref, = scratch
    k = pl.program_id(2)

    @pl.when(k == 0)
    def _():
        acc_ref[...] = jnp.dot(a_ref[...], w_ref[...], preferred_element_type=F32)

    @pl.when(jnp.logical_and(k > 0, k < nk - 1))
    def _():
        acc_ref[...] += jnp.dot(a_ref[...], w_ref[...], preferred_element_type=F32)

    @pl.when(k == nk - 1)
    def _():
        finish(acc_ref[...] + jnp.dot(a_ref[...], w_ref[...], preferred_element_type=F32))


def _matmul(a, w, n_cols=None, act=None, out_dtype=F32):
    m, kdim = a.shape
    n_cols = w.shape[1] if n_cols is None else n_cols
    tm = _tile(m, 1024, SUBLANES)
    tn = _tile(n_cols, 1024, LANES)
    tk = _tile(kdim, 4096, LANES)
    nk = kdim // tk
    return pl.pallas_call(
        functools.partial(_mm_kernel, nk=nk, act=act),
        grid=(m // tm, n_cols // tn, nk),
        in_specs=[pl.BlockSpec((tm, tk), lambda i, j, k: (i, k)),
                  pl.BlockSpec((tk, tn), lambda i, j, k: (k, j))],
        out_specs=pl.BlockSpec((tm, tn), lambda i, j, k: (i, j)),
        out_shape=jax.ShapeDtypeStruct((m, n_cols), out_dtype),
        scratch_shapes=[pltpu.VMEM((tm, tn), F32)] if nk > 1 else [],
        compiler_params=_params(("parallel", "parallel", "arbitrary")),
        name="matmul" + ("_" + act if act else ""),
    )(a, w)


def _ret_kernel(q_ref, k_ref, v_ref, g_ref, cos_ref, sin_ref, intra_ref, cross_ref, into_ref, whole_ref,
                s0_ref, o_ref, sout_ref, *scratch, nc, hb):
    c = pl.program_id(2)
    if nc > 1:
        s_scr, = scratch

        @pl.when(c == 0)
        def _():
            s_scr[...] = s0_ref[0]

    cos = cos_ref[...]
    sin = sin_ref[...]
    half = RET_DK // 2

    def rot(x):
        x1 = x[:, :half]
        x2 = x[:, half:]
        return jnp.concatenate([x1 * cos - x2 * sin, x2 * cos + x1 * sin], axis=1)

    for hh in range(hb):
        cols = slice(hh * RET_DK, (hh + 1) * RET_DK)
        qr = rot(q_ref[:, cols]).astype(BF16)
        kr = rot(k_ref[:, cols]) * (RET_DK ** -0.5)
        v = v_ref[:, cols].astype(BF16)
        s = s_scr[hh] if nc > 1 else s0_ref[0, hh]
        sc = _nt(qr, kr.astype(BF16)) * intra_ref[hh]
        o = (jnp.dot(sc.astype(BF16), v, preferred_element_type=F32)
             + jnp.dot(qr, s.astype(BF16), preferred_element_type=F32) * cross_ref[hh])
        s_new = s * whole_ref[hh] + _tn((kr * into_ref[hh]).astype(BF16), v)
        mu = jnp.mean(o, -1, keepdims=True)
        oc = o - mu
        var = jnp.mean(oc * oc, -1, keepdims=True)
        o_ref[:, cols] = (_silu(g_ref[:, cols]) * (oc * lax.rsqrt(var + EPS))).astype(o_ref.dtype)
        if nc > 1:
            s_scr[hh] = s_new

            @pl.when(c == nc - 1)
            def _():
                sout_ref[0, hh] = s_new
        else:
            sout_ref[0, hh] = s_new


def _retention(proj, row_off, b, l, q, hb, pos0, state, out_dtype):
    d = proj.shape[1] // 4
    h = d // RET_DK
    nc = l // q
    rb0 = row_off // q
    nhb = h // hb
    assert row_off % q == 0 and l % q == 0 and h % hb == 0
    half = RET_DK // 2
    inv = jnp.power(ROPE_BASE, -jnp.arange(half, dtype=F32) / half)
    ang = (pos0 + jnp.arange(l, dtype=F32))[:, None] * inv[None, :]
    cos, sin = jnp.cos(ang), jnp.sin(ang)
    idx = jnp.arange(q, dtype=F32)
    diff = idx[:, None] - idx[None, :]
    logg = jnp.log1p(-jnp.exp2(-5.0 - jnp.arange(h, dtype=F32)))
    intra = jnp.where(diff[None] >= 0, jnp.exp(logg[:, None, None] * jnp.maximum(diff, 0.0)[None]), 0.0)
    cross = jnp.exp(logg[:, None] * (idx[None, :] + 1.0))[:, :, None]
    into = jnp.exp(logg[:, None] * (q - 1.0 - idx[None, :]))[:, :, None]
    whole = jnp.exp(logg * q)[:, None, None]

    def part(p):
        return pl.BlockSpec((q, hb * RET_DK), lambda bi, hi, ci: (rb0 + bi * nc + ci, p * nhb + hi))

    tab = pl.BlockSpec((q, half), lambda bi, hi, ci: (ci, 0))
    col = pl.BlockSpec((hb, q, 1), lambda bi, hi, ci: (hi, 0, 0))
    st = pl.BlockSpec((1, hb, RET_DK, RET_DK), lambda bi, hi, ci: (bi, hi, 0, 0))
    return pl.pallas_call(
        functools.partial(_ret_kernel, nc=nc, hb=hb),
        grid=(b, nhb, nc),
        in_specs=[part(0), part(1), part(2), part(3), tab, tab,
                  pl.BlockSpec((hb, q, q), lambda bi, hi, ci: (hi, 0, 0)), col, col,
                  pl.BlockSpec((hb, 1, 1), lambda bi, hi, ci: (hi, 0, 0)), st],
        out_specs=[pl.BlockSpec((q, hb * RET_DK), lambda bi, hi, ci: (bi * nc + ci, hi)), st],
        out_shape=[jax.ShapeDtypeStruct((b * l, d), out_dtype),
                   jax.ShapeDtypeStruct((b, h, RET_DK, RET_DK), F32)],
        scratch_shapes=[pltpu.VMEM((hb, RET_DK, RET_DK), F32)] if nc > 1 else [],
        compiler_params=_params(("parallel", "parallel", "arbitrary")),
        name="retention",
    )(proj, proj, proj, proj, cos, sin, intra, cross, into, whole, state)


def _conv_kernel(cur_ref, prev_ref, st_ref, w_ref, b_ref, o_ref, *, blocks_per_seq):
    r = pl.program_id(0)
    cur = cur_ref[...]
    first = (r % blocks_per_seq) == 0
    prev = jnp.where(first, st_ref[0], prev_ref[...])
    w = w_ref[...]
    row8 = lax.broadcasted_iota(jnp.int32, prev.shape, 0)
    acc = b_ref[...] + cur * w[SSD_CONV - 1:SSD_CONV, :]
    for tap in range(SSD_CONV - 1):
        sh = SSD_CONV - 1 - tap
        rolled = pltpu.roll(cur, sh, 0)
        head = jnp.where(row8 < sh, pltpu.roll(prev, sh, 0), rolled[:SUBLANES])
        if cur.shape[0] > SUBLANES:
            shifted = jnp.concatenate([head, rolled[SUBLANES:]], axis=0)
        else:
            shifted = head
        acc = acc + shifted * w[tap:tap + 1, :]
    o_ref[...] = _silu(acc)


def _ssd_conv(proj, row_off, b, l, col0, conv_dim, state8, conv_w, conv_b):
    t = _tile(l, 256, SUBLANES)
    tc = _tile(math.gcd(col0, conv_dim), 2048, LANES)
    bps = l // t
    rb0 = row_off // t
    rb8 = row_off // SUBLANES
    t8 = t // SUBLANES
    j0 = col0 // tc
    return pl.pallas_call(
        functools.partial(_conv_kernel, blocks_per_seq=bps),
        grid=(b * bps, conv_dim // tc),
        in_specs=[pl.BlockSpec((t, tc), lambda r, j: (rb0 + r, j0 + j)),
                  pl.BlockSpec((SUBLANES, tc), lambda r, j: (jnp.maximum(rb8 + r * t8 - 1, 0), j0 + j)),
                  pl.BlockSpec((1, SUBLANES, tc), lambda r, j: (r // bps, 0, j)),
                  pl.BlockSpec((SSD_CONV, tc), lambda r, j: (0, j)),
                  pl.BlockSpec((1, tc), lambda r, j: (0, j))],
        out_specs=pl.BlockSpec((t, tc), lambda r, j: (r, j)),
        out_shape=jax.ShapeDtypeStruct((b * l, conv_dim), F32),
        compiler_params=_params(("parallel", "parallel")),
        name="ssd_conv",
    )(proj, proj, state8, conv_w, conv_b.reshape(1, conv_dim))


def _ssd_kernel(xs_ref, bm_ref, cm_ref, z_ref, dtc_ref, dtr_ref, biasc_ref, biasr_ref, alogc_ref, alogr_ref,
                dexp_ref, nw_ref, e_ref, s0_ref, y_ref, sout_ref, *scratch, nc, q, gb):
    c = pl.program_id(2)
    if nc > 1:
        s_scr, yi_scr = scratch

        @pl.when(c == 0)
        def _():
            s_scr[...] = s0_ref[0]
    else:
        yi_scr, = scratch

    ii = lax.broadcasted_iota(jnp.int32, (q, q), 0)
    jj = lax.broadcasted_iota(jnp.int32, (q, q), 1)
    causal = ii >= jj
    tri = causal.astype(F32)
    triu = (jj >= ii).astype(F32)
    lane = lax.broadcasted_iota(jnp.int32, (q, LANES), 1)
    e = e_ref[...]

    def expand(a):
        hi = a.astype(BF16)
        lo = (a - hi.astype(F32)).astype(BF16)
        return (jnp.dot(hi, e, preferred_element_type=F32) + jnp.dot(lo, e, preferred_element_type=F32))

    for gg in range(gb):
        gcols = slice(gg * SSD_GROUP_W, (gg + 1) * SSD_GROUP_W)
        ncols = slice(gg * SSD_STATE, (gg + 1) * SSD_STATE)
        dtc = _softplus(dtc_ref[gg, 0] + biasc_ref[gg])
        lac = dtc * (-jnp.exp(alogc_ref[gg]))
        dtr = _softplus(dtr_ref[gg, 0] + biasr_ref[gg])
        lar = dtr * (-jnp.exp(alogr_ref[gg]))
        cum_c = jnp.dot(tri, lac, preferred_element_type=F32, precision=HIGHEST)
        cum_r = jnp.dot(lar, triu, preferred_element_type=F32, precision=HIGHEST)
        last = cum_c[q - 1:q, :]

        xs = xs_ref[:, gcols]
        xdt = xs * expand(dtc)
        xdt_b = xdt.astype(BF16)
        bm = bm_ref[:, ncols].astype(BF16)
        cm = cm_ref[:, ncols].astype(BF16)
        cb = _nt(cm, bm)
        for p in range(SSD_REP // 2):
            x2 = xdt_b[:, p * LANES:(p + 1) * LANES]
            ys = []
            for hh in (2 * p, 2 * p + 1):
                seg = cum_c[:, hh:hh + 1] - cum_r[hh:hh + 1, :]
                dec = jnp.where(causal, jnp.exp(jnp.where(causal, seg, 0.0)), 0.0)
                ys.append(jnp.dot((cb * dec).astype(BF16), x2, preferred_element_type=F32))
            yi_scr[:, p * LANES:(p + 1) * LANES] = jnp.where(lane < SSD_HEADDIM, ys[0], ys[1])

        s = s_scr[gg] if nc > 1 else s0_ref[0, gg]
        y = yi_scr[...] + _nt(cm, s.astype(BF16)) * expand(jnp.exp(cum_c))
        wj = jnp.exp(last - cum_c)
        upd = _tn((xdt * expand(wj)).astype(BF16), bm)
        elast = jnp.exp(last)
        dst = s_scr if nc > 1 else sout_ref.at[0]
        for hh in range(SSD_REP):
            rows = slice(hh * SSD_HEADDIM, (hh + 1) * SSD_HEADDIM)
            dst[gg, rows, :] = s[rows, :] * elast[:, hh:hh + 1] + upd[rows, :]

        y = y + xs * dexp_ref[gg]
        y = y * _silu(z_ref[:, gcols])
        y = y * lax.rsqrt(jnp.mean(y * y, -1, keepdims=True) + EPS)
        y_ref[:, gcols] = (y * nw_ref[gg]).astype(y_ref.dtype)

    if nc > 1:
        @pl.when(c == nc - 1)
        def _():
            sout_ref[0] = s_scr[...]


def _ssd_scan(proj, xact, dt_raw, row_off, b, l, q, gb, state, dt_bias, a_log, d_skip, norm_w, out_dtype):
    conv_dim = xact.shape[1]
    heads = dt_raw.shape[1]
    g = heads // SSD_REP
    dinner = heads * SSD_HEADDIM
    gn = g * SSD_STATE
    ngb = g // gb
    assert conv_dim == dinner + 2 * gn and l % q == 0 and row_off % q == 0 and g % gb == 0
    nc = l // q
    nblk = b * nc
    rb0 = row_off // q
    dtc = dt_raw.reshape(nblk, q, g, SSD_REP).transpose(2, 0, 1, 3)
    dtr = dt_raw.reshape(nblk, q, g, SSD_REP).transpose(2, 0, 3, 1)
    vec_c = lambda a: a.reshape(g, 1, SSD_REP)
    vec_r = lambda a: a.reshape(g, SSD_REP, 1)
    dexp = jnp.repeat(d_skip, SSD_HEADDIM).reshape(g, 1, SSD_GROUP_W)
    nw = norm_w.reshape(g, 1, SSD_GROUP_W)
    e = (jnp.arange(SSD_GROUP_W)[None, :] // SSD_HEADDIM == jnp.arange(SSD_REP)[:, None]).astype(BF16)
    st = state.reshape(b, g, SSD_GROUP_W, SSD_STATE)
    nb0 = dinner // (gb * SSD_STATE)

    blk = lambda bi, gi, ci: bi * nc + ci
    vc = pl.BlockSpec((gb, 1, SSD_REP), lambda bi, gi, ci: (gi, 0, 0))
    vr = pl.BlockSpec((gb, SSD_REP, 1), lambda bi, gi, ci: (gi, 0, 0))
    vg = pl.BlockSpec((gb, 1, SSD_GROUP_W), lambda bi, gi, ci: (gi, 0, 0))
    stspec = pl.BlockSpec((1, gb, SSD_GROUP_W, SSD_STATE), lambda bi, gi, ci: (bi, gi, 0, 0))
    scratch = [pltpu.VMEM((q, SSD_GROUP_W), F32)]
    if nc > 1:
        scratch = [pltpu.VMEM((gb, SSD_GROUP_W, SSD_STATE), F32)] + scratch
    y, s_out = pl.pallas_call(
        functools.partial(_ssd_kernel, nc=nc, q=q, gb=gb),
        grid=(b, ngb, nc),
        in_specs=[pl.BlockSpec((q, gb * SSD_GROUP_W), lambda bi, gi, ci: (blk(bi, gi, ci), gi)),
                  pl.BlockSpec((q, gb * SSD_STATE), lambda bi, gi, ci: (blk(bi, gi, ci), nb0 + gi)),
                  pl.BlockSpec((q, gb * SSD_STATE), lambda bi, gi, ci: (blk(bi, gi, ci), nb0 + ngb + gi)),
                  pl.BlockSpec((q, gb * SSD_GROUP_W), lambda bi, gi, ci: (rb0 + blk(bi, gi, ci), gi)),
                  pl.BlockSpec((gb, 1, q, SSD_REP), lambda bi, gi, ci: (gi, blk(bi, gi, ci), 0, 0)),
                  pl.BlockSpec((gb, 1, SSD_REP, q), lambda bi, gi, ci: (gi, blk(bi, gi, ci), 0, 0)),
                  vc, vr, vc, vr, vg, vg,
                  pl.BlockSpec((SSD_REP, SSD_GROUP_W), lambda bi, gi, ci: (0, 0)),
                  stspec],
        out_specs=[pl.BlockSpec((q, gb * SSD_GROUP_W), lambda bi, gi, ci: (blk(bi, gi, ci), gi)), stspec],
        out_shape=[jax.ShapeDtypeStruct((b * l, dinner), out_dtype),
                   jax.ShapeDtypeStruct(st.shape, F32)],
        scratch_shapes=scratch,
        compiler_params=_params(("parallel", "parallel", "arbitrary")),
        name="ssd_scan",
    )(xact, xact, xact, proj, dtc, dtr, vec_c(dt_bias), vec_r(dt_bias), vec_c(a_log), vec_r(a_log),
      dexp, nw, e, st)
    return y, s_out.reshape(state.shape)


def _mlstm_kernel(q_ref, k_ref, v_ref, og_ref, gc_ref, gr_ref, bc_ref, br_ref, c0_ref, n0_ref, m0_ref,
                  h_ref, cout_ref, nout_ref, mout_ref, *scratch, nc, q, hb):
    c = pl.program_id(2)
    if nc > 1:
        c_scr, n_scr, m_scr = scratch

        @pl.when(c == 0)
        def _():
            c_scr[...] = c0_ref[0]
            n_scr[...] = n0_ref[0]
            m_scr[...] = m0_ref[0]

    ii = lax.broadcasted_iota(jnp.int32, (q, q), 0)
    jj = lax.broadcasted_iota(jnp.int32, (q, q), 1)
    causal = ii >= jj
    tri = causal.astype(F32)
    triu = (jj >= ii).astype(F32)

    for hh in range(hb):
        qcols = slice(hh * ML_DQK, (hh + 1) * ML_DQK)
        vcols = slice(hh * ML_DV, (hh + 1) * ML_DV)
        gcol = gc_ref[hh, 0] + bc_ref[hh]
        grow = gr_ref[hh, 0] + br_ref[hh]
        ig_c = gcol[:, 0:1]
        ig_r = grow[0:1, :]
        lf_c = _log_sigmoid(gcol)
        lf_r = _log_sigmoid(grow)
        bcum_c = jnp.dot(tri, lf_c, preferred_element_type=F32, precision=HIGHEST)[:, 1:2]
        bcum_r = jnp.dot(lf_r, triu, preferred_element_type=F32, precision=HIGHEST)[1:2, :]

        if nc > 1:
            cs, ns, m_prev = c_scr[hh], n_scr[hh], m_scr[hh]
        else:
            cs, ns, m_prev = c0_ref[0, hh], n0_ref[0, hh], m0_ref[0, hh]
        logw = jnp.where(causal, bcum_c - bcum_r + ig_r, -jnp.inf)
        logs = bcum_c + m_prev
        mt = jnp.maximum(logs, jnp.max(logw, axis=1, keepdims=True))
        qf = q_ref[:, qcols]
        kf = k_ref[:, qcols] * (ML_DQK ** -0.5)
        qb = qf.astype(BF16)
        v = v_ref[:, vcols].astype(BF16)
        qk = _nt(qb, kf.astype(BF16)) * jnp.exp(logw - mt)
        ws = jnp.exp(logs - mt)
        num = (jnp.dot(qk.astype(BF16), v, preferred_element_type=F32)
               + jnp.dot(qb, cs.astype(BF16), preferred_element_type=F32) * ws)
        den = jnp.sum(qk, axis=1, keepdims=True) + jnp.sum(qf * ns, axis=1, keepdims=True) * ws
        hc = num / jnp.maximum(jnp.abs(den), jnp.exp(-mt))
        h_ref[:, vcols] = (jax.nn.sigmoid(og_ref[:, vcols]) * hc).astype(h_ref.dtype)

        m_new = mt[q - 1:q, :]
        blast = bcum_c[q - 1:q, :]
        wl = jnp.exp(blast - bcum_c + ig_c - m_new)
        sl = jnp.exp(blast + m_prev - m_new)
        kw = kf * wl
        c_new = cs * sl + _tn(kw.astype(BF16), v)
        n_new = ns * sl + jnp.sum(kw, axis=0, keepdims=True)
        if nc > 1:
            c_scr[hh] = c_new
            n_scr[hh] = n_new
            m_scr[hh] = m_new

            @pl.when(c == nc - 1)
            def _():
                cout_ref[0, hh] = c_new
                nout_ref[0, hh] = n_new
                mout_ref[0, hh] = m_new
        else:
            cout_ref[0, hh] = c_new
            nout_ref[0, hh] = n_new
            mout_ref[0, hh] = m_new


def _mlstm(proj, graw, row_off, b, l, q, hb, c_state, n_state, m_state, b_gates, out_dtype):
    h = graw.shape[1] // 2
    d = h * ML_DV
    nc = l // q
    nblk = b * nc
    rb0 = row_off // q
    nhb = h // hb
    assert row_off % q == 0 and l % q == 0 and h % hb == 0
    g4 = graw.reshape(nblk, q, 2, h)
    gc = g4.transpose(3, 0, 1, 2)
    gr = g4.transpose(3, 0, 2, 1)
    bc = b_gates.reshape(2, h).T.reshape(h, 1, 2)
    br = b_gates.reshape(2, h).T.reshape(h, 2, 1)
    kb0 = nhb
    vb0 = (2 * h * ML_DQK) // (hb * ML_DV)
    ob0 = vb0 + nhb

    blk = lambda bi, hi, ci: bi * nc + ci
    cst = pl.BlockSpec((1, hb, ML_DQK, ML_DV), lambda bi, hi, ci: (bi, hi, 0, 0))
    nst = pl.BlockSpec((1, hb, 1, ML_DQK), lambda bi, hi, ci: (bi, hi, 0, 0))
    mst = pl.BlockSpec((1, hb, 1, 1), lambda bi, hi, ci: (bi, hi, 0, 0))
    scratch = []
    if nc > 1:
        scratch = [pltpu.VMEM((hb, ML_DQK, ML_DV), F32), pltpu.VMEM((hb, 1, ML_DQK), F32),
                   pltpu.VMEM((hb, 1, 1), F32)]
    hs, c_out, n_out, m_out = pl.pallas_call(
        functools.partial(_mlstm_kernel, nc=nc, q=q, hb=hb),
        grid=(b, nhb, nc),
        in_specs=[pl.BlockSpec((q, hb * ML_DQK), lambda bi, hi, ci: (rb0 + blk(bi, hi, ci), hi)),
                  pl.BlockSpec((q, hb * ML_DQK), lambda bi, hi, ci: (rb0 + blk(bi, hi, ci), kb0 + hi)),
                  pl.BlockSpec((q, hb * ML_DV), lambda bi, hi, ci: (rb0 + blk(bi, hi, ci), vb0 + hi)),
                  pl.BlockSpec((q, hb * ML_DV), lambda bi, hi, ci: (rb0 + blk(bi, hi, ci), ob0 + hi)),
                  pl.BlockSpec((hb, 1, q, 2), lambda bi, hi, ci: (hi, blk(bi, hi, ci), 0, 0)),
                  pl.BlockSpec((hb, 1, 2, q), lambda bi, hi, ci: (hi, blk(bi, hi, ci), 0, 0)),
                  pl.BlockSpec((hb, 1, 2), lambda bi, hi, ci: (hi, 0, 0)),
                  pl.BlockSpec((hb, 2, 1), lambda bi, hi, ci: (hi, 0, 0)),
                  cst, nst, mst],
        out_specs=[pl.BlockSpec((q, hb * ML_DV), lambda bi, hi, ci: (blk(bi, hi, ci), hi)), cst, nst, mst],
        out_shape=[jax.ShapeDtypeStruct((b * l, d), out_dtype),
                   jax.ShapeDtypeStruct((b, h, ML_DQK, ML_DV), F32),
                   jax.ShapeDtypeStruct((b, h, 1, ML_DQK), F32),
                   jax.ShapeDtypeStruct((b, h, 1, 1), F32)],
        scratch_shapes=scratch,
        compiler_params=_params(("parallel", "parallel", "arbitrary")),
        name="mlstm",
    )(proj, proj, proj, proj, gc, gr, bc, br, c_state, n_state.reshape(b, h, 1, ML_DQK),
      m_state.reshape(b, h, 1, 1))
    return hs, c_out, n_out.reshape(b, h, ML_DQK), m_out.reshape(b, h)


def _chunk(l, pref):
    return _tile(l, pref, SUBLANES)


def _pad_cols(w):
    return jnp.pad(w, ((0, 0), (0, -w.shape[1] % LANES)))


def kernel(x_prompt, x_sample, state_ret, state_ssm, state_conv, state_mlstm_C, state_mlstm_n, state_mlstm_m, norm_mix_pre, norm_mix_post, norm_ffn_pre, norm_ffn_post, ret_w_in, ret_w_out, ssd_w_in, ssd_conv_w, ssd_conv_b, ssd_dt_bias, ssd_a_log, ssd_d, ssd_norm_w, ssd_w_out, ml_w_in, ml_b_gates, ml_w_out, ffn_w_up, ffn_w_down):
    bp, lp, d = x_prompt.shape
    bs, ls, _ = x_sample.shape
    depth = norm_mix_pre.shape[0]
    mp, ms = bp * lp, bs * ls
    assert lp >= SSD_CONV - 1 and ls >= SSD_CONV - 1
    dinner = ssd_w_out.shape[1]
    ssd_heads = dinner // SSD_HEADDIM
    ssd_groups = ssd_heads // SSD_REP
    conv_dim = ssd_conv_w.shape[2]
    ml_heads = d // ML_DV
    ml_qk = ml_heads * ML_DQK
    ret_heads = d // RET_DK
    qs = _chunk(ls, 256)
    dt = x_prompt.dtype

    x = jnp.concatenate([x_prompt.reshape(mp, d), x_sample.reshape(ms, d)], axis=0)
    hn = _prenorm(x, norm_mix_pre[0])
    new = {k: ([], []) for k in ("ret", "ssm", "conv", "c", "n", "m")}

    for i in range(depth):
        j = i // 3
        kind = i % 3
        if kind == 0:
            proj = _matmul(hn, _to_bf16(ret_w_in, j))
            zero = jnp.zeros((bp, ret_heads, RET_DK, RET_DK), F32)
            op, sp = _retention(proj, 0, bp, lp, _chunk(lp, 256), _tile(ret_heads, 4, 1), 0.0, zero, BF16)
            os_, ss = _retention(proj, mp, bs, ls, qs, ret_heads, float(PAST_LEN), state_ret[j], F32)
            new["ret"][0].append(sp)
            new["ret"][1].append(ss)
            w_out = _to_bf16(ret_w_out, j)
        elif kind == 1:
            w_in = _to_bf16(ssd_w_in, j)
            main = dinner + conv_dim
            proj = _matmul(hn, w_in, n_cols=main)
            dt_raw = _matmul(hn, _pad_cols(w_in[:, main:]))[:, :ssd_heads]
            outs = []
            for g_i, (off, b, l, q, gb, st_ssm, st_conv) in enumerate((
                    (0, bp, lp, _chunk(lp, 128), 1, jnp.zeros((bp,) + state_ssm.shape[2:], F32),
                     jnp.zeros((bp, SSD_CONV - 1, conv_dim), F32)),
                    (mp, bs, ls, qs, ssd_groups, state_ssm[j], state_conv[j]))):
                st8 = jnp.pad(st_conv, ((0, 0), (SUBLANES - (SSD_CONV - 1), 0), (0, 0)))
                xact = _ssd_conv(proj, off, b, l, dinner, conv_dim, st8, ssd_conv_w[j], ssd_conv_b[j])
                y, s_new = _ssd_scan(proj, xact, dt_raw[off:off + b * l], off, b, l, q, gb, st_ssm,
                                     ssd_dt_bias[j], ssd_a_log[j], ssd_d[j], ssd_norm_w[j],
                                     BF16 if g_i == 0 else F32)
                outs.append(y)
                new["ssm"][g_i].append(s_new)
                xbc = proj[off:off + b * l, dinner:].reshape(b, l, conv_dim)
                new["conv"][g_i].append(xbc[:, l - (SSD_CONV - 1):])
            op, os_ = outs
            w_out = _to_bf16(ssd_w_out, j)
        else:
            w_in = _to_bf16(ml_w_in, j)
            main = 2 * ml_qk + 2 * d
            proj = _matmul(hn, w_in, n_cols=main)
            graw = _matmul(hn, _pad_cols(w_in[:, main:]))[:, :2 * ml_heads]
            op, cp, np_, mp_ = _mlstm(proj, graw[:mp], 0, bp, lp, _chunk(lp, 128), 1,
                                      jnp.zeros((bp,) + state_mlstm_C.shape[2:], F32),
                                      jnp.zeros((bp,) + state_mlstm_n.shape[2:], F32),
                                      jnp.zeros((bp,) + state_mlstm_m.shape[2:], F32), ml_b_gates[j], BF16)
            os_, cs_, ns_, ms_ = _mlstm(proj, graw[mp:], mp, bs, ls, qs, ml_heads, state_mlstm_C[j],
                                        state_mlstm_n[j], state_mlstm_m[j], ml_b_gates[j], F32)
            for key, vp, vs in (("c", cp, cs_), ("n", np_, ns_), ("m", mp_, ms_)):
                new[key][0].append(vp)
                new[key][1].append(vs)
            w_out = _to_bf16(ml_w_out, j)

        mixed = jnp.concatenate([op, os_.astype(BF16)], axis=0)
        out = _matmul(mixed, w_out)
        x, hn = _post_residual_prenorm(out, x, norm_mix_post[i], norm_ffn_pre[i])
        up = _matmul(hn, _to_bf16(ffn_w_up, i), act="relu2", out_dtype=BF16)
        f = _matmul(up, _to_bf16(ffn_w_down, i))
        x, hn = _post_residual_prenorm(f, x, norm_ffn_post[i], norm_mix_pre[i + 1] if i + 1 < depth else None)

    y_prompt = x[:mp].reshape(bp, lp, d)
    y_sample = x[mp:].reshape(bs, ls, d)
    st = lambda key, g_i: jnp.stack(new[key][g_i]).astype(dt)
    return (y_prompt, y_sample, st("ret", 0), st("ret", 1), st("ssm", 0), st("ssm", 1),
            st("conv", 0), st("conv", 1), st("c", 0), st("c", 1), st("n", 0), st("n", 1),
            st("m", 0), st("m", 1))
```

```python
import functools
import math

import jax
import jax.numpy as jnp
from jax import lax
from jax.experimental import pallas as pl
from jax.experimental.pallas import tpu as pltpu

F32 = jnp.float32
BF16 = jnp.bfloat16
HIGHEST = lax.Precision.HIGHEST

EPS = 1e-6
PAST_LEN = 16384
ROPE_BASE = 10000.0
RET_DK = 256
SSD_HEADDIM = 64
SSD_STATE = 128
SSD_REP = 16
SSD_CONV = 4
SSD_GROUP_W = SSD_REP * SSD_HEADDIM
ML_DQK = 256
ML_DV = 512

LANES = 128
SUBLANES = 8
BF16_ROWS = 16
VMEM_LIMIT_BYTES = 52 * 1024 * 1024
CAST_BLOCK_BYTES = 8 * 1024 * 1024


def _params(sem):
    return pltpu.CompilerParams(dimension_semantics=sem, vmem_limit_bytes=VMEM_LIMIT_BYTES)


def _tile(dim, pref, mult):
    best = None
    t = mult
    while t <= min(dim, pref):
        if dim % t == 0:
            best = t
        t += mult
    assert best is not None, (dim, pref, mult)
    return best


def _nt(a, b):
    return lax.dot_general(a, b, (((1,), (1,)), ((), ())), preferred_element_type=F32)


def _tn(a, b):
    return lax.dot_general(a, b, (((0,), (0,)), ((), ())), preferred_element_type=F32)


def _silu(x):
    return x * jax.nn.sigmoid(x)


def _softplus(x):
    return jnp.maximum(x, 0.0) + jnp.log1p(jnp.exp(-jnp.abs(x)))


def _log_sigmoid(x):
    return -_softplus(-x)


SHORT_CHUNK = 16


def _cumsum_rows(x):
    q = x.shape[0]
    if q <= SHORT_CHUNK:
        row = lax.broadcasted_iota(jnp.int32, x.shape, 0)
        acc = jnp.zeros_like(x)
        for j in range(q):
            acc = acc + jnp.where(row >= j, x[j:j + 1, :], 0.0)
        return acc
    tri = (lax.broadcasted_iota(jnp.int32, (q, q), 0) >= lax.broadcasted_iota(jnp.int32, (q, q), 1)).astype(F32)
    return jnp.dot(tri, x, preferred_element_type=F32, precision=HIGHEST)


def _cumsum_cols(x):
    q = x.shape[1]
    if q <= SHORT_CHUNK:
        lane = lax.broadcasted_iota(jnp.int32, x.shape, 1)
        acc = jnp.zeros_like(x)
        for j in range(q):
            acc = acc + jnp.where(lane >= j, x[:, j:j + 1], 0.0)
        return acc
    triu = (lax.broadcasted_iota(jnp.int32, (q, q), 1) >= lax.broadcasted_iota(jnp.int32, (q, q), 0)).astype(F32)
    return jnp.dot(x, triu, preferred_element_type=F32, precision=HIGHEST)


def _cast_kernel(w_ref, o_ref):
    o_ref[...] = w_ref[...].astype(o_ref.dtype)


def _to_bf16(w_stack, j):
    _, r, c = w_stack.shape
    tr = _tile(r, max(BF16_ROWS, CAST_BLOCK_BYTES // (4 * c)), BF16_ROWS)
    return pl.pallas_call(
        _cast_kernel,
        grid=(r // tr,),
        in_specs=[pl.BlockSpec((None, tr, c), lambda i: (j, i, 0))],
        out_specs=pl.BlockSpec((tr, c), lambda i: (i, 0)),
        out_shape=jax.ShapeDtypeStruct((r, c), BF16),
        compiler_params=_params(("parallel",)),
        name="cast_bf16",
    )(w_stack)


def _rms(x, w):
    return x * lax.rsqrt(jnp.mean(x * x, -1, keepdims=True) + EPS) * w


def _two_part_specs(tm, d, n1):
    return [pl.BlockSpec((tm, d), lambda i: (jnp.minimum(i, n1 - 1), 0)),
            pl.BlockSpec((tm, d), lambda i: (jnp.maximum(i - n1, 0), 0))]


def _prenorm2_kernel(x1_ref, x2_ref, w_ref, h_ref, *, n1):
    x = jnp.where(pl.program_id(0) < n1, x1_ref[...], x2_ref[...])
    h_ref[...] = _rms(x, w_ref[...]).astype(h_ref.dtype)


def _prenorm2(x1, x2, w):
    (m1, d), m2 = x1.shape, x2.shape[0]
    tm = _tile(math.gcd(m1, m2), 512, SUBLANES)
    n1 = m1 // tm
    return pl.pallas_call(
        functools.partial(_prenorm2_kernel, n1=n1),
        grid=((m1 + m2) // tm,),
        in_specs=_two_part_specs(tm, d, n1) + [pl.BlockSpec((1, d), lambda i: (0, 0))],
        out_specs=pl.BlockSpec((tm, d), lambda i: (i, 0)),
        out_shape=jax.ShapeDtypeStruct((m1 + m2, d), BF16),
        compiler_params=_params(("parallel",)),
        name="prenorm2",
    )(x1, x2, w.reshape(1, d))


def _post_kernel(*refs, n1, with_next):
    refs = list(refs)
    o_ref = refs.pop(0)
    if n1 is None:
        x = refs.pop(0)[...]
    else:
        x1_ref, x2_ref = refs.pop(0), refs.pop(0)
        x = jnp.where(pl.program_id(0) < n1, x1_ref[...], x2_ref[...])
    wpost_ref = refs.pop(0)
    xn = x + _rms(o_ref[...].astype(F32), wpost_ref[...])
    if with_next:
        wpre_ref, xn_ref, hn_ref = refs
        hn_ref[...] = _rms(xn, wpre_ref[...]).astype(hn_ref.dtype)
    else:
        xn_ref, = refs
    xn_ref[...] = xn


def _post_residual_prenorm(o, x_parts, w_post, w_pre_next, row_off=0, rows=None):
    d = o.shape[1]
    rows = o.shape[0] if rows is None else rows
    two = len(x_parts) == 2
    tm = _tile(math.gcd(math.gcd(rows, row_off), x_parts[0].shape[0]), 256, SUBLANES)
    r0 = row_off // tm
    row = pl.BlockSpec((tm, d), lambda i: (r0 + i, 0))
    out_row = pl.BlockSpec((tm, d), lambda i: (i, 0))
    vec = pl.BlockSpec((1, d), lambda i: (0, 0))
    n1 = x_parts[0].shape[0] // tm if two else None
    assert not (two and row_off)
    with_next = w_pre_next is not None
    xs = jax.ShapeDtypeStruct((rows, d), F32)
    res = pl.pallas_call(
        functools.partial(_post_kernel, n1=n1, with_next=with_next),
        grid=(rows // tm,),
        in_specs=[row] + (_two_part_specs(tm, d, n1) if two else [row]) + [vec] + ([vec] if with_next else []),
        out_specs=[out_row, out_row] if with_next else [out_row],
        out_shape=[xs, jax.ShapeDtypeStruct((rows, d), BF16)] if with_next else [xs],
        compiler_params=_params(("parallel",)),
        name="post_residual_prenorm",
    )(o, *x_parts, w_post.reshape(1, d), *([w_pre_next.reshape(1, d)] if with_next else []))
    return (res[0], res[1]) if with_next else (res[0], None)


def _mm_kernel(*refs, nk, act, n1):
    if n1 is None:
        a_ref, w_ref, o_ref, *scratch = refs
        load_a = lambda: a_ref[...]
    else:
        a1_ref, a2_ref, w_ref, o_ref, *scratch = refs
        load_a = lambda: jnp.where(pl.program_id(0) < n1, a1_ref[...], a2_ref[...])

    def finish(r):
        if act == "relu2":
            r = jnp.square(jnp.maximum(r, 0.0))
        o_ref[...] = r.astype(o_ref.dtype)

    def prod():
        return jnp.dot(load_a(), w_ref[...], preferred_element_type=F32)

    if nk == 1:
        finish(prod())
        return
    acc_ref, = scratch
    k = pl.program_id(2)

    @pl.when(k == 0)
    def _():
        acc_ref[...] = prod()

    @pl.when(jnp.logical_and(k > 0, k < nk - 1))
    def _():
        acc_ref[...] += prod()

    @pl.when(k == nk - 1)
    def _():
        finish(acc_ref[...] + prod())


def _matmul(a, w, n_cols=None, act=None, out_dtype=F32, a2=None):
    m1, kdim = a.shape
    m2 = 0 if a2 is None else a2.shape[0]
    m = m1 + m2
    n_cols = w.shape[1] if n_cols is None else n_cols
    tm = _tile(math.gcd(m1, m2), 512, SUBLANES) if m2 else _tile(m1, 1024, SUBLANES)
    tn = _tile(n_cols, 1024, LANES)
    tk = _tile(kdim, 4096, LANES)
    nk = kdim // tk
    n1 = m1 // tm if m2 else None
    if m2:
        a_specs = [pl.BlockSpec((tm, tk), lambda i, j, k: (jnp.minimum(i, n1 - 1), k)),
                   pl.BlockSpec((tm, tk), lambda i, j, k: (jnp.maximum(i - n1, 0), k))]
    else:
        a_specs = [pl.BlockSpec((tm, tk), lambda i, j, k: (i, k))]
    return pl.pallas_call(
        functools.partial(_mm_kernel, nk=nk, act=act, n1=n1),
        grid=(m // tm, n_cols // tn, nk),
        in_specs=a_specs + [pl.BlockSpec((tk, tn), lambda i, j, k: (k, j))],
        out_specs=pl.BlockSpec((tm, tn), lambda i, j, k: (i, j)),
        out_shape=jax.ShapeDtypeStruct((m, n_cols), out_dtype),
        scratch_shapes=[pltpu.VMEM((tm, tn), F32)] if nk > 1 else [],
        compiler_params=_params(("parallel", "parallel", "arbitrary")),
        name="matmul" + ("_" + act if act else ""),
    )(*([a, a2] if m2 else [a]), w)


def _ret_kernel(q_ref, k_ref, v_ref, g_ref, cos_ref, sin_ref, intra_ref, cross_ref, into_ref, whole_ref,
                s0_ref, *rest, nc, hb, aliased):
    if aliased:
        rest = rest[1:]
    o_ref, sout_ref, *scratch = rest
    c = pl.program_id(2)
    if nc > 1:
        s_scr, = scratch

        @pl.when(c == 0)
        def _():
            s_scr[...] = s0_ref[0]

    cos = cos_ref[...]
    sin = sin_ref[...]
    half = RET_DK // 2

    def rot(x):
        x1 = x[:, :half]
        x2 = x[:, half:]
        return jnp.concatenate([x1 * cos - x2 * sin, x2 * cos + x1 * sin], axis=1)

    for hh in range(hb):
        cols = slice(hh * RET_DK, (hh + 1) * RET_DK)
        qr = rot(q_ref[:, cols]).astype(BF16)
        kr = rot(k_ref[:, cols]) * (RET_DK ** -0.5)
        v = v_ref[:, cols].astype(BF16)
        s = s_scr[hh] if nc > 1 else s0_ref[0, hh]
        sc = _nt(qr, kr.astype(BF16)) * intra_ref[hh]
        o = (jnp.dot(sc.astype(BF16), v, preferred_element_type=F32)
             + jnp.dot(qr, s.astype(BF16), preferred_element_type=F32) * cross_ref[hh])
        s_new = s * whole_ref[hh] + _tn((kr * into_ref[hh]).astype(BF16), v)
        mu = jnp.mean(o, -1, keepdims=True)
        oc = o - mu
        var = jnp.mean(oc * oc, -1, keepdims=True)
        o_ref[:, cols] = (_silu(g_ref[:, cols]) * (oc * lax.rsqrt(var + EPS))).astype(o_ref.dtype)
        if nc > 1:
            s_scr[hh] = s_new

            @pl.when(c == nc - 1)
            def _():
                sout_ref[0, hh] = s_new
        else:
            sout_ref[0, hh] = s_new


def _retention(proj, row_off, b, l, q, hb, pos0, state_stack, j_in, out_stack, j_out, n_stack, out_dtype):
    d = proj.shape[1] // 4
    h = d // RET_DK
    nc = l // q
    rb0 = row_off // q
    nhb = h // hb
    assert row_off % q == 0 and l % q == 0 and h % hb == 0
    half = RET_DK // 2
    inv = jnp.power(ROPE_BASE, -jnp.arange(half, dtype=F32) / half)
    ang = (pos0 + jnp.arange(l, dtype=F32))[:, None] * inv[None, :]
    cos, sin = jnp.cos(ang), jnp.sin(ang)
    idx = jnp.arange(q, dtype=F32)
    diff = idx[:, None] - idx[None, :]
    logg = jnp.log1p(-jnp.exp2(-5.0 - jnp.arange(h, dtype=F32)))
    intra = jnp.where(diff[None] >= 0, jnp.exp(logg[:, None, None] * jnp.maximum(diff, 0.0)[None]), 0.0)
    cross = jnp.exp(logg[:, None] * (idx[None, :] + 1.0))[:, :, None]
    into = jnp.exp(logg[:, None] * (q - 1.0 - idx[None, :]))[:, :, None]
    whole = jnp.exp(logg * q)[:, None, None]

    def part(p):
        return pl.BlockSpec((q, hb * RET_DK), lambda bi, hi, ci: (rb0 + bi * nc + ci, p * nhb + hi))

    tab = pl.BlockSpec((q, half), lambda bi, hi, ci: (ci, 0))
    col = pl.BlockSpec((hb, q, 1), lambda bi, hi, ci: (hi, 0, 0))
    st = pl.BlockSpec((None, 1, hb, RET_DK, RET_DK), lambda bi, hi, ci: (j_in, bi, hi, 0, 0))
    st_out = pl.BlockSpec((None, 1, hb, RET_DK, RET_DK), lambda bi, hi, ci: (j_out, bi, hi, 0, 0))
    aliased = out_stack is not None
    out_index = 1
    in_specs = [part(0), part(1), part(2), part(3), tab, tab,
                pl.BlockSpec((hb, q, q), lambda bi, hi, ci: (hi, 0, 0)), col, col,
                pl.BlockSpec((hb, 1, 1), lambda bi, hi, ci: (hi, 0, 0)), st]
    operands = [proj, proj, proj, proj, cos, sin, intra, cross, into, whole, state_stack]
    if aliased:
        in_specs.append(pl.BlockSpec(memory_space=pl.ANY))
        operands.append(out_stack)
    return pl.pallas_call(
        functools.partial(_ret_kernel, nc=nc, hb=hb, aliased=aliased),
        grid=(b, nhb, nc),
        in_specs=in_specs,
        out_specs=[pl.BlockSpec((q, hb * RET_DK), lambda bi, hi, ci: (bi * nc + ci, hi)), st_out],
        out_shape=[jax.ShapeDtypeStruct((b * l, d), out_dtype),
                   jax.ShapeDtypeStruct((n_stack, b, h, RET_DK, RET_DK), F32)],
        scratch_shapes=[pltpu.VMEM((hb, RET_DK, RET_DK), F32)] if nc > 1 else [],
        input_output_aliases={len(operands) - 1: out_index} if aliased else {},
        compiler_params=_params(("parallel", "parallel", "arbitrary")),
        name="retention",
    )(*operands)


def _conv_kernel(cur_ref, prev_ref, st_ref, w_ref, b_ref, o_ref, *, blocks_per_seq):
    r = pl.program_id(0)
    cur = cur_ref[...]
    first = (r % blocks_per_seq) == 0
    prev = jnp.where(first, st_ref[0], prev_ref[...])
    w = w_ref[...]
    row8 = lax.broadcasted_iota(jnp.int32, prev.shape, 0)
    acc = b_ref[...] + cur * w[SSD_CONV - 1:SSD_CONV, :]
    for tap in range(SSD_CONV - 1):
        sh = SSD_CONV - 1 - tap
        rolled = pltpu.roll(cur, sh, 0)
        head = jnp.where(row8 < sh, pltpu.roll(prev, sh, 0), rolled[:SUBLANES])
        if cur.shape[0] > SUBLANES:
            shifted = jnp.concatenate([head, rolled[SUBLANES:]], axis=0)
        else:
            shifted = head
        acc = acc + shifted * w[tap:tap + 1, :]
    o_ref[...] = _silu(acc)


def _ssd_conv(proj, row_off, b, l, col0, conv_dim, state8, conv_w, conv_b):
    t = _tile(l, 256, SUBLANES)
    tc = _tile(math.gcd(col0, conv_dim), 2048, LANES)
    bps = l // t
    rb0 = row_off // t
    rb8 = row_off // SUBLANES
    t8 = t // SUBLANES
    j0 = col0 // tc
    return pl.pallas_call(
        functools.partial(_conv_kernel, blocks_per_seq=bps),
        grid=(b * bps, conv_dim // tc),
        in_specs=[pl.BlockSpec((t, tc), lambda r, j: (rb0 + r, j0 + j)),
                  pl.BlockSpec((SUBLANES, tc), lambda r, j: (jnp.maximum(rb8 + r * t8 - 1, 0), j0 + j)),
                  pl.BlockSpec((1, SUBLANES, tc), lambda r, j: (r // bps, 0, j)),
                  pl.BlockSpec((SSD_CONV, tc), lambda r, j: (0, j)),
                  pl.BlockSpec((1, tc), lambda r, j: (0, j))],
        out_specs=pl.BlockSpec((t, tc), lambda r, j: (r, j)),
        out_shape=jax.ShapeDtypeStruct((b * l, conv_dim), F32),
        compiler_params=_params(("parallel", "parallel")),
        name="ssd_conv",
    )(proj, proj, state8, conv_w, conv_b.reshape(1, conv_dim))


def _conv8_kernel(cur_ref, st_ref, w_ref, b_ref, o_ref):
    cur = cur_ref[...]
    n = cur.shape[0]
    st = st_ref[...].reshape(n, cur.shape[1])
    w = w_ref[...]
    rmod = jnp.bitwise_and(lax.broadcasted_iota(jnp.int32, cur.shape, 0), SUBLANES - 1)
    acc = b_ref[...] + cur * w[SSD_CONV - 1:SSD_CONV, :]
    for tap in range(SSD_CONV - 1):
        sh = SSD_CONV - 1 - tap
        shifted = jnp.where(rmod < sh, pltpu.roll(st, (n - SUBLANES + sh) % n, 0), pltpu.roll(cur, sh, 0))
        acc = acc + shifted * w[tap:tap + 1, :]
    o_ref[...] = _silu(acc)


def _ssd_conv8(proj, row_off, b, col0, conv_dim, state8, conv_w, conv_b):
    nb = _tile(b, 16, 1)
    t = nb * SUBLANES
    tc = _tile(math.gcd(col0, conv_dim), 2048, LANES)
    assert row_off % t == 0
    rb0 = row_off // t
    j0 = col0 // tc
    return pl.pallas_call(
        _conv8_kernel,
        grid=(b // nb, conv_dim // tc),
        in_specs=[pl.BlockSpec((t, tc), lambda r, j: (rb0 + r, j0 + j)),
                  pl.BlockSpec((nb, SUBLANES, tc), lambda r, j: (r, 0, j)),
                  pl.BlockSpec((SSD_CONV, tc), lambda r, j: (0, j)),
                  pl.BlockSpec((1, tc), lambda r, j: (0, j))],
        out_specs=pl.BlockSpec((t, tc), lambda r, j: (r, j)),
        out_shape=jax.ShapeDtypeStruct((b * SUBLANES, conv_dim), F32),
        compiler_params=_params(("parallel", "parallel")),
        name="ssd_conv8",
    )(proj, state8, conv_w, conv_b.reshape(1, conv_dim))


def _ssd_kernel(xs_ref, bm_ref, cm_ref, z_ref, dtc_ref, dtr_ref, biasc_ref, biasr_ref, alogc_ref, alogr_ref,
                dexp_ref, nw_ref, e_ref, s0_ref, y_ref, sout_ref, *scratch, nc, q, gb):
    c = pl.program_id(2)
    if nc > 1:
        s_scr, = scratch

        @pl.when(c == 0)
        def _():
            s_scr[...] = s0_ref[0]

    ii = lax.broadcasted_iota(jnp.int32, (q, q), 0)
    jj = lax.broadcasted_iota(jnp.int32, (q, q), 1)
    causal = ii >= jj
    lane =lax.broadcasted_iota(jnp.int32, (q, LANES), 1)
    e = e_ref[...]

    def expand(a):
        hi = a.astype(BF16)
        lo = (a - hi.astype(F32)).astype(BF16)
        return (jnp.dot(hi, e, preferred_element_type=F32) + jnp.dot(lo, e, preferred_element_type=F32))

    pending = []
    for gg in range(gb):
        gcols = slice(gg * SSD_GROUP_W, (gg + 1) * SSD_GROUP_W)
        ncols = slice(gg * SSD_STATE, (gg + 1) * SSD_STATE)
        dtc = _softplus(dtc_ref[gg, 0] + biasc_ref[gg])
        lac = dtc * (-jnp.exp(alogc_ref[gg]))
        dtr = _softplus(dtr_ref[gg, 0] + biasr_ref[gg])
        lar = dtr * (-jnp.exp(alogr_ref[gg]))
        cum_c = _cumsum_rows(lac)
        cum_r = _cumsum_cols(lar)
        last = cum_c[q - 1:q, :]

        xs = xs_ref[:, gcols]
        xdt = xs * expand(dtc)
        xdt_b = xdt.astype(BF16)
        bm = bm_ref[:, ncols].astype(BF16)
        cm = cm_ref[:, ncols].astype(BF16)
        cb = _nt(cm, bm)
        pairs = []
        for p in range(SSD_REP // 2):
            x2 = xdt_b[:, p * LANES:(p + 1) * LANES]
            ys = []
            for hh in (2 * p, 2 * p + 1):
                seg = cum_c[:, hh:hh + 1] - cum_r[hh:hh + 1, :]
                dec = jnp.where(causal, jnp.exp(jnp.where(causal, seg, 0.0)), 0.0)
                ys.append(jnp.dot((cb * dec).astype(BF16), x2, preferred_element_type=F32))
            pairs.append(jnp.where(lane < SSD_HEADDIM, ys[0], ys[1]))
        y = jnp.concatenate(pairs, axis=1)

        s = s_scr[gg] if nc > 1 else s0_ref[0, gg]
        y = y + _nt(cm, s.astype(BF16)) * expand(jnp.exp(cum_c))
        wj = jnp.exp(last - cum_c)
        xdtw = (xdt * expand(wj)).astype(BF16)
        y = y + xs * dexp_ref[gg]
        y = y * _silu(z_ref[:, gcols])
        y = y * lax.rsqrt(jnp.mean(y * y, -1, keepdims=True) + EPS)
        pending.append(((y * nw_ref[gg]).astype(y_ref.dtype), xdtw, bm, jnp.exp(last)))

    for gg, (y, xdtw, bm, elast) in enumerate(pending):
        y_ref[:, gg * SSD_GROUP_W:(gg + 1) * SSD_GROUP_W] = y
        s = s_scr[gg] if nc > 1 else s0_ref[0, gg]
        upd = _tn(xdtw, bm)
        dst = s_scr if nc > 1 else sout_ref.at[0]
        for hh in range(SSD_REP):
            rows = slice(hh * SSD_HEADDIM, (hh + 1) * SSD_HEADDIM)
            dst[gg, rows, :] = s[rows, :] * elast[:, hh:hh + 1] + upd[rows, :]

    if nc > 1:
        @pl.when(c == nc - 1)
        def _():
            sout_ref[0] = s_scr[...]


def _ssd_scan(proj, xact, dt_raw, row_off, b, l, q, gb, state, dt_bias, a_log, d_skip, norm_w, out_dtype):
    conv_dim = xact.shape[1]
    heads = dt_raw.shape[1]
    g = heads // SSD_REP
    dinner = heads * SSD_HEADDIM
    gn = g * SSD_STATE
    ngb = g // gb
    assert conv_dim == dinner + 2 * gn and l % q == 0 and row_off % q == 0 and g % gb == 0
    nc = l // q
    nblk = b * nc
    rb0 = row_off // q
    dtc = dt_raw.reshape(nblk, q, g, SSD_REP).transpose(2, 0, 1, 3)
    dtr = dt_raw.reshape(nblk, q, g, SSD_REP).transpose(2, 0, 3, 1)
    vec_c = lambda a: a.reshape(g, 1, SSD_REP)
    vec_r = lambda a: a.reshape(g, SSD_REP, 1)
    dexp = jnp.repeat(d_skip, SSD_HEADDIM).reshape(g, 1, SSD_GROUP_W)
    nw = norm_w.reshape(g, 1, SSD_GROUP_W)
    e = (jnp.arange(SSD_GROUP_W)[None, :] // SSD_HEADDIM == jnp.arange(SSD_REP)[:, None]).astype(BF16)
    st = state.reshape(b, g, SSD_GROUP_W, SSD_STATE)
    nb0 = dinner // (gb * SSD_STATE)

    blk = lambda bi, gi, ci: bi * nc + ci
    vc = pl.BlockSpec((gb, 1, SSD_REP), lambda bi, gi, ci: (gi, 0, 0))
    vr = pl.BlockSpec((gb, SSD_REP, 1), lambda bi, gi, ci: (gi, 0, 0))
    vg = pl.BlockSpec((gb, 1, SSD_GROUP_W), lambda bi, gi, ci: (gi, 0, 0))
    stspec = pl.BlockSpec((1, gb, SSD_GROUP_W, SSD_STATE), lambda bi, gi, ci: (bi, gi, 0, 0))
    y, s_out = pl.pallas_call(
        functools.partial(_ssd_kernel, nc=nc, q=q, gb=gb),
        grid=(b, ngb, nc),
        in_specs=[pl.BlockSpec((q, gb * SSD_GROUP_W), lambda bi, gi, ci: (blk(bi, gi, ci), gi)),
                  pl.BlockSpec((q, gb * SSD_STATE), lambda bi, gi, ci: (blk(bi, gi, ci), nb0 + gi)),
                  pl.BlockSpec((q, gb * SSD_STATE), lambda bi, gi, ci: (blk(bi, gi, ci), nb0 + ngb + gi)),
                  pl.BlockSpec((q, gb * SSD_GROUP_W), lambda bi, gi, ci: (rb0 + blk(bi, gi, ci), gi)),
                  pl.BlockSpec((gb, 1, q, SSD_REP), lambda bi, gi, ci: (gi, blk(bi, gi, ci), 0, 0)),
                  pl.BlockSpec((gb, 1, SSD_REP, q), lambda bi, gi, ci: (gi, blk(bi, gi, ci), 0, 0)),
                  vc, vr, vc, vr, vg, vg,
                  pl.BlockSpec((SSD_REP, SSD_GROUP_W), lambda bi, gi, ci: (0, 0)),
                  stspec],
        out_specs=[pl.BlockSpec((q, gb * SSD_GROUP_W), lambda bi, gi, ci: (blk(bi, gi, ci), gi)), stspec],
        out_shape=[jax.ShapeDtypeStruct((b * l, dinner), out_dtype),
                   jax.ShapeDtypeStruct(st.shape, F32)],
        scratch_shapes=[pltpu.VMEM((gb, SSD_GROUP_W, SSD_STATE), F32)] if nc > 1 else [],
        compiler_params=_params(("parallel", "parallel", "arbitrary")),
        name="ssd_scan",
    )(xact, xact, xact, proj, dtc, dtr, vec_c(dt_bias), vec_r(dt_bias), vec_c(a_log), vec_r(a_log),
      dexp, nw, e, st)
    return y, s_out.reshape(state.shape)


def _mlstm_kernel(q_ref, k_ref, v_ref, og_ref, gc_ref, gr_ref, bc_ref, br_ref, c0_ref, n0_ref, m0_ref,
                  h_ref, cout_ref, nout_ref, mout_ref, *scratch, nc, q, hb):
    c = pl.program_id(2)
    if nc > 1:
        c_scr, n_scr, m_scr = scratch

        @pl.when(c == 0)
        def _():
            c_scr[...] = c0_ref[0]
            n_scr[...] = n0_ref[0]
            m_scr[...] = m0_ref[0]

    ii = lax.broadcasted_iota(jnp.int32, (q, q), 0)
    jj = lax.broadcasted_iota(jnp.int32, (q, q), 1)
    causal = ii >= jj

    pending = []
    for hh in range(hb):
        qcols = slice(hh * ML_DQK, (hh + 1) * ML_DQK)
        vcols = slice(hh * ML_DV, (hh + 1) * ML_DV)
        gcol = gc_ref[hh, 0] + bc_ref[hh]
        grow = gr_ref[hh, 0] + br_ref[hh]
        ig_c = gcol[:, 0:1]
        ig_r = grow[0:1, :]
        lf_c = _log_sigmoid(gcol)
        lf_r = _log_sigmoid(grow)
        bcum_c = _cumsum_rows(lf_c)[:, 1:2]
        bcum_r = _cumsum_cols(lf_r)[1:2, :]

        if nc > 1:
            cs, ns, m_prev = c_scr[hh], n_scr[hh], m_scr[hh]
        else:
            cs, ns, m_prev = c0_ref[0, hh], n0_ref[0, hh], m0_ref[0, hh]
        logw = jnp.where(causal, bcum_c - bcum_r + ig_r, -jnp.inf)
        logs = bcum_c + m_prev
        mt = jnp.maximum(logs, jnp.max(logw, axis=1, keepdims=True))
        qf = q_ref[:, qcols]
        kf = k_ref[:, qcols] * (ML_DQK ** -0.5)
        qb = qf.astype(BF16)
        v = v_ref[:, vcols].astype(BF16)
        qk = _nt(qb, kf.astype(BF16)) * jnp.exp(logw - mt)
        ws = jnp.exp(logs - mt)
        num = (jnp.dot(qk.astype(BF16), v, preferred_element_type=F32)
               + jnp.dot(qb, cs.astype(BF16), preferred_element_type=F32) * ws)
        den = jnp.sum(qk, axis=1, keepdims=True) + jnp.sum(qf * ns, axis=1, keepdims=True) * ws
        hc = num / jnp.maximum(jnp.abs(den), jnp.exp(-mt))
        h_out = (jax.nn.sigmoid(og_ref[:, vcols]) * hc).astype(h_ref.dtype)

        m_new = mt[q - 1:q, :]
        blast = bcum_c[q - 1:q, :]
        wl = jnp.exp(blast - bcum_c + ig_c - m_new)
        sl = jnp.exp(blast + m_prev - m_new)
        kw = kf * wl
        n_new = ns * sl + jnp.sum(kw, axis=0, keepdims=True)
        pending.append((h_out, kw.astype(BF16), v, sl, n_new, m_new))

    for hh, (h_out, kw, v, sl, n_new, m_new) in enumerate(pending):
        h_ref[:, hh * ML_DV:(hh + 1) * ML_DV] = h_out
        cs = c_scr[hh] if nc > 1 else c0_ref[0, hh]
        c_new = cs * sl + _tn(kw, v)
        if nc > 1:
            c_scr[hh] = c_new
            n_scr[hh] = n_new
            m_scr[hh] = m_new

            @pl.when(c == nc - 1)
            def _():
                cout_ref[0, hh] = c_new
                nout_ref[0, hh] = n_new
                mout_ref[0, hh] = m_new
        else:
            cout_ref[0, hh] = c_new
            nout_ref[0, hh] = n_new
            mout_ref[0, hh] = m_new


def _mlstm(proj, graw, row_off, b, l, q, hb, c_state, n_state, m_state, b_gates, out_dtype):
    h = graw.shape[1] // 2
    d = h * ML_DV
    nc = l // q
    nblk = b * nc
    rb0 = row_off // q
    nhb = h // hb
    assert row_off % q == 0 and l % q == 0 and h % hb == 0
    g4 = graw.reshape(nblk, q, 2, h)
    gc = g4.transpose(3, 0, 1, 2)
    gr = g4.transpose(3, 0, 2, 1)
    bc = b_gates.reshape(2, h).T.reshape(h, 1, 2)
    br = b_gates.reshape(2, h).T.reshape(h, 2, 1)
    kb0 = nhb
    vb0 = (2 * h * ML_DQK) // (hb * ML_DV)
    ob0 = vb0 + nhb

    blk = lambda bi, hi, ci: bi * nc + ci
    cst = pl.BlockSpec((1, hb, ML_DQK, ML_DV), lambda bi, hi, ci: (bi, hi, 0, 0))
    nst = pl.BlockSpec((1, hb, 1, ML_DQK), lambda bi, hi, ci: (bi, hi, 0, 0))
    mst = pl.BlockSpec((1, hb, 1, 1), lambda bi, hi, ci: (bi, hi, 0, 0))
    scratch = []
    if nc > 1:
        scratch = [pltpu.VMEM((hb, ML_DQK, ML_DV), F32), pltpu.VMEM((hb, 1, ML_DQK), F32),
                   pltpu.VMEM((hb, 1, 1), F32)]
    hs, c_out, n_out, m_out = pl.pallas_call(
        functools.partial(_mlstm_kernel, nc=nc, q=q, hb=hb),
        grid=(b, nhb, nc),
        in_specs=[pl.BlockSpec((q, hb * ML_DQK), lambda bi, hi, ci: (rb0 + blk(bi, hi, ci), hi)),
                  pl.BlockSpec((q, hb * ML_DQK), lambda bi, hi, ci: (rb0 + blk(bi, hi, ci), kb0 + hi)),
                  pl.BlockSpec((q, hb * ML_DV), lambda bi, hi, ci: (rb0 + blk(bi, hi, ci), vb0 + hi)),
                  pl.BlockSpec((q, hb * ML_DV), lambda bi, hi, ci: (rb0 + blk(bi, hi, ci), ob0 + hi)),
                  pl.BlockSpec((hb, 1, q, 2), lambda bi, hi, ci: (hi, blk(bi, hi, ci), 0, 0)),
                  pl.BlockSpec((hb, 1, 2, q), lambda bi, hi, ci: (hi, blk(bi, hi, ci), 0, 0)),
                  pl.BlockSpec((hb, 1, 2), lambda bi, hi, ci: (hi, 0, 0)),
                  pl.BlockSpec((hb, 2, 1), lambda bi, hi, ci: (hi, 0, 0)),
                  cst, nst, mst],
        out_specs=[pl.BlockSpec((q, hb * ML_DV), lambda bi, hi, ci: (blk(bi, hi, ci), hi)), cst, nst, mst],
        out_shape=[jax.ShapeDtypeStruct((b * l, d), out_dtype),
                   jax.ShapeDtypeStruct((b, h, ML_DQK, ML_DV), F32),
                   jax.ShapeDtypeStruct((b, h, 1, ML_DQK), F32),
                   jax.ShapeDtypeStruct((b, h, 1, 1), F32)],
        scratch_shapes=scratch,
        compiler_params=_params(("parallel", "parallel", "arbitrary")),
        name="mlstm",
    )(proj, proj, proj, proj, gc, gr, bc, br, c_state, n_state.reshape(b, h, 1, ML_DQK),
      m_state.reshape(b, h, 1, 1))
    return hs, c_out, n_out.reshape(b, h, ML_DQK), m_out.reshape(b, h)


def _chunk(l, pref):
    return _tile(l, pref, SUBLANES)


def _pad_cols(w):
    return jnp.pad(w, ((0, 0), (0, -w.shape[1] % LANES)))


def kernel(x_prompt, x_sample, state_ret, state_ssm, state_conv, state_mlstm_C, state_mlstm_n, state_mlstm_m, norm_mix_pre, norm_mix_post, norm_ffn_pre, norm_ffn_post, ret_w_in, ret_w_out, ssd_w_in, ssd_conv_w, ssd_conv_b, ssd_dt_bias, ssd_a_log, ssd_d, ssd_norm_w, ssd_w_out, ml_w_in, ml_b_gates, ml_w_out, ffn_w_up, ffn_w_down):
    bp, lp, d = x_prompt.shape
    bs, ls, _ = x_sample.shape
    depth = norm_mix_pre.shape[0]
    mp, ms = bp * lp, bs * ls
    assert lp >= SSD_CONV - 1 and ls >= SSD_CONV - 1
    dinner = ssd_w_out.shape[1]
    ssd_heads = dinner // SSD_HEADDIM
    ssd_groups = ssd_heads // SSD_REP
    conv_dim = ssd_conv_w.shape[2]
    ml_heads = d // ML_DV
    ml_qk = ml_heads * ML_DQK
    ret_heads = d // RET_DK
    n_ret = state_ret.shape[0]
    qs = _chunk(ls, 256)
    dt = x_prompt.dtype

    x_parts = (x_prompt.reshape(mp, d), x_sample.reshape(ms, d))
    hn = _prenorm2(*x_parts, norm_mix_pre[0])
    new = {k: ([], []) for k in ("ssm", "conv", "c", "n", "m")}
    ret_p = ret_s = None
    zero_ret = jnp.zeros((1, bp, ret_heads, RET_DK, RET_DK), F32)

    for i in range(depth):
        j = i // 3
        kind = i % 3
        if kind == 0:
            proj = _matmul(hn, _to_bf16(ret_w_in, j))
            op, ret_p = _retention(proj, 0, bp, lp, _chunk(lp, 256), _tile(ret_heads, 4, 1), 0.0,
                                   zero_ret, 0, ret_p, j, n_ret, BF16)
            os_, ret_s = _retention(proj, mp, bs, ls, qs, ret_heads, float(PAST_LEN),
                                    state_ret, j, ret_s, j, n_ret, BF16)
            w_out = _to_bf16(ret_w_out, j)
        elif kind == 1:
            w_in = _to_bf16(ssd_w_in, j)
            main = dinner + conv_dim
            proj = _matmul(hn, w_in, n_cols=main)
            dt_raw = _matmul(hn, _pad_cols(w_in[:, main:]))[:, :ssd_heads]
            outs = []
            for g_i, (off, b, l, q, gb, st_ssm, st_conv) in enumerate((
                    (0, bp, lp, _chunk(lp, 128), _tile(ssd_groups, 2, 1), jnp.zeros((bp,) + state_ssm.shape[2:], F32),
                     jnp.zeros((bp, SSD_CONV - 1, conv_dim), F32)),
                    (mp, bs, ls, qs, ssd_groups, state_ssm[j], state_conv[j]))):
                st8 = jnp.pad(st_conv, ((0, 0), (SUBLANES - (SSD_CONV - 1), 0), (0, 0)))
                if l == SUBLANES:
                    xact = _ssd_conv8(proj, off, b, dinner, conv_dim, st8, ssd_conv_w[j], ssd_conv_b[j])
                else:
                    xact = _ssd_conv(proj, off, b, l, dinner, conv_dim, st8, ssd_conv_w[j], ssd_conv_b[j])
                y, s_new = _ssd_scan(proj, xact, dt_raw[off:off + b * l], off, b, l, q, gb, st_ssm,
                                     ssd_dt_bias[j], ssd_a_log[j], ssd_d[j], ssd_norm_w[j], BF16)
                outs.append(y)
                new["ssm"][g_i].append(s_new)
                tails = [lax.slice(proj, (off + (bi + 1) * l - (SSD_CONV - 1), dinner), (off + (bi + 1) * l, main))
                         for bi in range(b)] if b <= 8 else None
                if tails is None:
                    tails = lax.slice(proj, (off, dinner), (off + b * l, main)).reshape(b, l, conv_dim)
                    new["conv"][g_i].append(tails[:, l - (SSD_CONV - 1):])
                else:
                    new["conv"][g_i].append(jnp.stack(tails))
            op, os_ = outs
            w_out = _to_bf16(ssd_w_out, j)
        else:
            w_in = _to_bf16(ml_w_in, j)
            main = 2 * ml_qk + 2 * d
            proj = _matmul(hn, w_in, n_cols=main)
            graw = _matmul(hn, _pad_cols(w_in[:, main:]))[:, :2 * ml_heads]
            op, cp, np_, mp_ = _mlstm(proj, graw[:mp], 0, bp, lp, _chunk(lp, 128), _tile(ml_heads, 2, 1),
                                      jnp.zeros((bp,) + state_mlstm_C.shape[2:], F32),
                                      jnp.zeros((bp,) + state_mlstm_n.shape[2:], F32),
                                      jnp.zeros((bp,) + state_mlstm_m.shape[2:], F32), ml_b_gates[j], BF16)
            os_, cs_, ns_, ms_ = _mlstm(proj, graw[mp:], mp, bs, ls, qs, ml_heads, state_mlstm_C[j],
                                        state_mlstm_n[j], state_mlstm_m[j], ml_b_gates[j], BF16)
            for key, vp, vs in (("c", cp, cs_), ("n", np_, ns_), ("m", mp_, ms_)):
                new[key][0].append(vp)
                new[key][1].append(vs)
            w_out = _to_bf16(ml_w_out, j)

        out = _matmul(op, w_out, a2=os_, out_dtype=BF16)
        x, hn = _post_residual_prenorm(out, x_parts, norm_mix_post[i], norm_ffn_pre[i])
        up = _matmul(hn, _to_bf16(ffn_w_up, i), act="relu2", out_dtype=BF16)
        f = _matmul(up, _to_bf16(ffn_w_down, i), out_dtype=BF16)
        if i + 1 < depth:
            x, hn = _post_residual_prenorm(f, (x,), norm_ffn_post[i], norm_mix_pre[i + 1])
            x_parts = (x,)
        else:
            y_prompt, _ = _post_residual_prenorm(f, (x,), norm_ffn_post[i], None, 0, mp)
            y_sample, _ = _post_residual_prenorm(f, (x,), norm_ffn_post[i], None, mp, ms)

    st = lambda key, g_i: jnp.stack(new[key][g_i]).astype(dt)
    return (y_prompt.reshape(bp, lp, d), y_sample.reshape(bs, ls, d), ret_p.astype(dt), ret_s.astype(dt),
            st("ssm", 0), st("ssm", 1), st("conv", 0), st("conv", 1), st("c", 0), st("c", 1),
            st("n", 0), st("n", 1), st("m", 0), st("m", 1))
```

```python
import functools
import math

import jax
import jax.numpy as jnp
from jax import lax
from jax.experimental import pallas as pl
from jax.experimental.pallas import tpu as pltpu

F32 = jnp.float32
BF16 = jnp.bfloat16
HIGHEST = lax.Precision.HIGHEST

EPS = 1e-6
PAST_LEN = 16384
ROPE_BASE = 10000.0
RET_DK = 256
SSD_HEADDIM = 64
SSD_STATE = 128
SSD_REP = 16
SSD_CONV = 4
SSD_GROUP_W = SSD_REP * SSD_HEADDIM
ML_DQK = 256
ML_DV = 512

LANES = 128
SUBLANES = 8
BF16_ROWS = 16
VMEM_LIMIT_BYTES = 58 * 1024 * 1024
CAST_BLOCK_BYTES = 8 * 1024 * 1024


def _params(sem):
    return pltpu.CompilerParams(dimension_semantics=sem, vmem_limit_bytes=VMEM_LIMIT_BYTES)


def _tile(dim, pref, mult):
    best = None
    t = mult
    while t <= min(dim, pref):
        if dim % t == 0:
            best = t
        t += mult
    assert best is not None, (dim, pref, mult)
    return best


def _nt(a, b):
    return lax.dot_general(a, b, (((1,), (1,)), ((), ())), preferred_element_type=F32)


def _tn(a, b):
    return lax.dot_general(a, b, (((0,), (0,)), ((), ())), preferred_element_type=F32)


def _silu(x):
    return x * jax.nn.sigmoid(x)


def _softplus(x):
    return jnp.maximum(x, 0.0) + jnp.log1p(jnp.exp(-jnp.abs(x)))


def _log_sigmoid(x):
    return -_softplus(-x)


SHORT_CHUNK = 16


def _cumsum_rows(x):
    q = x.shape[0]
    if q <= SHORT_CHUNK:
        row = lax.broadcasted_iota(jnp.int32, x.shape, 0)
        acc = jnp.zeros_like(x)
        for j in range(q):
            acc = acc + jnp.where(row >= j, x[j:j + 1, :], 0.0)
        return acc
    tri = (lax.broadcasted_iota(jnp.int32, (q, q), 0) >= lax.broadcasted_iota(jnp.int32, (q, q), 1)).astype(F32)
    return jnp.dot(tri, x, preferred_element_type=F32, precision=HIGHEST)


def _cumsum_cols(x):
    q = x.shape[1]
    if q <= SHORT_CHUNK:
        lane = lax.broadcasted_iota(jnp.int32, x.shape, 1)
        acc = jnp.zeros_like(x)
        for j in range(q):
            acc = acc + jnp.where(lane >= j, x[:, j:j + 1], 0.0)
        return acc
    triu = (lax.broadcasted_iota(jnp.int32, (q, q), 1) >= lax.broadcasted_iota(jnp.int32, (q, q), 0)).astype(F32)
    return jnp.dot(x, triu, preferred_element_type=F32, precision=HIGHEST)


def _cast_kernel(w_ref, o_ref):
    o_ref[...] = w_ref[...].astype(o_ref.dtype)


def _to_bf16(w_stack, j):
    _, r, c = w_stack.shape
    tr = _tile(r, max(BF16_ROWS, CAST_BLOCK_BYTES // (4 * c)), BF16_ROWS)
    return pl.pallas_call(
        _cast_kernel,
        grid=(r // tr,),
        in_specs=[pl.BlockSpec((None, tr, c), lambda i: (j, i, 0))],
        out_specs=pl.BlockSpec((tr, c), lambda i: (i, 0)),
        out_shape=jax.ShapeDtypeStruct((r, c), BF16),
        compiler_params=_params(("parallel",)),
        name="cast_bf16",
    )(w_stack)


def _rms(x, w):
    return x * lax.rsqrt(jnp.mean(x * x, -1, keepdims=True) + EPS) * w


def _two_part_specs(tm, d, n1):
    return [pl.BlockSpec((tm, d), lambda i: (jnp.minimum(i, n1 - 1), 0)),
            pl.BlockSpec((tm, d), lambda i: (jnp.maximum(i - n1, 0), 0))]


def _prenorm2_kernel(x1_ref, x2_ref, w_ref, h_ref, *, n1):
    x = jnp.where(pl.program_id(0) < n1, x1_ref[...], x2_ref[...])
    h_ref[...] = _rms(x, w_ref[...]).astype(h_ref.dtype)


def _prenorm2(x1, x2, w):
    (m1, d), m2 = x1.shape, x2.shape[0]
    tm = _tile(math.gcd(m1, m2), 512, SUBLANES)
    n1 = m1 // tm
    return pl.pallas_call(
        functools.partial(_prenorm2_kernel, n1=n1),
        grid=((m1 + m2) // tm,),
        in_specs=_two_part_specs(tm, d, n1) + [pl.BlockSpec((1, d), lambda i: (0, 0))],
        out_specs=pl.BlockSpec((tm, d), lambda i: (i, 0)),
        out_shape=jax.ShapeDtypeStruct((m1 + m2, d), BF16),
        compiler_params=_params(("parallel",)),
        name="prenorm2",
    )(x1, x2, w.reshape(1, d))


def _post_kernel(*refs, n1, with_next):
    refs = list(refs)
    o_ref = refs.pop(0)
    if n1 is None:
        x = refs.pop(0)[...]
    else:
        x1_ref, x2_ref = refs.pop(0), refs.pop(0)
        x = jnp.where(pl.program_id(0) < n1, x1_ref[...], x2_ref[...])
    wpost_ref = refs.pop(0)
    xn = x + _rms(o_ref[...].astype(F32), wpost_ref[...])
    if with_next:
        wpre_ref, xn_ref, hn_ref = refs
        hn_ref[...] = _rms(xn, wpre_ref[...]).astype(hn_ref.dtype)
    else:
        xn_ref, = refs
    xn_ref[...] = xn


def _post_residual_prenorm(o, x_parts, w_post, w_pre_next, row_off=0, rows=None):
    d = o.shape[1]
    rows = o.shape[0] if rows is None else rows
    two = len(x_parts) == 2
    tm = _tile(math.gcd(math.gcd(rows, row_off), x_parts[0].shape[0]), 256, SUBLANES)
    r0 = row_off // tm
    row = pl.BlockSpec((tm, d), lambda i: (r0 + i, 0))
    out_row = pl.BlockSpec((tm, d), lambda i: (i, 0))
    vec = pl.BlockSpec((1, d), lambda i: (0, 0))
    n1 = x_parts[0].shape[0] // tm if two else None
    assert not (two and row_off)
    with_next = w_pre_next is not None
    xs = jax.ShapeDtypeStruct((rows, d), F32)
    res = pl.pallas_call(
        functools.partial(_post_kernel, n1=n1, with_next=with_next),
        grid=(rows // tm,),
        in_specs=[row] + (_two_part_specs(tm, d, n1) if two else [row]) + [vec] + ([vec] if with_next else []),
        out_specs=[out_row, out_row] if with_next else [out_row],
        out_shape=[xs, jax.ShapeDtypeStruct((rows, d), BF16)] if with_next else [xs],
        compiler_params=_params(("parallel",)),
        name="post_residual_prenorm",
    )(o, *x_parts, w_post.reshape(1, d), *([w_pre_next.reshape(1, d)] if with_next else []))
    return (res[0], res[1]) if with_next else (res[0], None)


def _mm_kernel(*refs, nk, act, n1, n_side):
    refs = list(refs)
    if n1 is None:
        a_ref = refs.pop(0)
        load_a = lambda: a_ref[...]
    else:
        a1_ref, a2_ref = refs.pop(0), refs.pop(0)
        load_a = lambda: jnp.where(pl.program_id(0) < n1, a1_ref[...], a2_ref[...])
    w_ref = refs.pop(0)
    side_in = [refs.pop(0) for _ in range(n_side)]
    o_ref = refs.pop(0)
    side_out = [refs.pop(0) for _ in range(n_side)]
    scratch = refs

    for src, dst in zip(side_in, side_out):
        dst[...] = src[...].astype(dst.dtype)

    def finish(r):
        if act == "relu2":
            r = jnp.square(jnp.maximum(r, 0.0))
        o_ref[...] = r.astype(o_ref.dtype)

    def prod():
        return jnp.dot(load_a(), w_ref[...], preferred_element_type=F32)

    if nk == 1:
        finish(prod())
        return
    acc_ref, = scratch
    k = pl.program_id(2)

    @pl.when(k == 0)
    def _():
        acc_ref[...] = prod()

    @pl.when(jnp.logical_and(k > 0, k < nk - 1))
    def _():
        acc_ref[...] += prod()

    @pl.when(k == nk - 1)
    def _():
        finish(acc_ref[...] + prod())


def _matmul(a, w, n_cols=None, act=None, out_dtype=F32, a2=None, side=()):
    m1, kdim = a.shape
    m2 = 0 if a2 is None else a2.shape[0]
    m = m1 + m2
    n_cols = w.shape[1] if n_cols is None else n_cols
    tm = _tile(math.gcd(m1, m2), 512, SUBLANES) if m2 else _tile(m1, 1024, SUBLANES)
    tk = _tile(kdim, 4096, LANES)
    nk = kdim // tk
    out_bytes = jnp.dtype(out_dtype).itemsize

    def side_rows(r, steps):
        return min(t for t in range(BF16_ROWS, r + 1, BF16_ROWS) if r % t == 0 and r // t <= steps)

    def vmem_bytes(tn):
        steps = (m // tm) * (n_cols // tn) * nk
        blocks = (2 if m2 else 1) * tm * tk * 2 + tk * tn * 2 + tm * tn * out_bytes
        blocks += sum(side_rows(ws.shape[1], steps) * ws.shape[2] * (4 + 2) for ws, _ in side)
        return 2 * blocks + (tm * tn * 4 if nk > 1 else 0) + 2 * tm * tn * 4

    tn = next(t for t in (_tile(n_cols, p, LANES) for p in (1024, 512, 256, 128))
              if vmem_bytes(t) <= VMEM_LIMIT_BYTES or t == _tile(n_cols, 128, LANES))
    nj = n_cols // tn
    steps = (m // tm) * nj * nk
    n1 = m1 // tm if m2 else None
    if m2:
        a_specs = [pl.BlockSpec((tm, tk), lambda i, j, k: (jnp.minimum(i, n1 - 1), k)),
                   pl.BlockSpec((tm, tk), lambda i, j, k: (jnp.maximum(i - n1, 0), k))]
    else:
        a_specs = [pl.BlockSpec((tm, tk), lambda i, j, k: (i, k))]

    side_in_specs, side_out_specs, side_shapes = [], [], []
    for w_stack, js in side:
        _, r, c = w_stack.shape
        tr = side_rows(r, steps)
        last = r // tr - 1

        def block(i, j, k, last=last):
            return jnp.minimum((i * nj + j) * nk + k, last)

        side_in_specs.append(pl.BlockSpec((None, tr, c), lambda i, j, k, js=js, block=block: (js, block(i, j, k), 0)))
        side_out_specs.append(pl.BlockSpec((tr, c), lambda i, j, k, block=block: (block(i, j, k), 0)))
        side_shapes.append(jax.ShapeDtypeStruct((r, c), BF16))

    res = pl.pallas_call(
        functools.partial(_mm_kernel, nk=nk, act=act, n1=n1, n_side=len(side)),
        grid=(m // tm, nj, nk),
        in_specs=a_specs + [pl.BlockSpec((tk, tn), lambda i, j, k: (k, j))] + side_in_specs,
        out_specs=[pl.BlockSpec((tm, tn), lambda i, j, k: (i, j))] + side_out_specs,
        out_shape=[jax.ShapeDtypeStruct((m, n_cols), out_dtype)] + side_shapes,
        scratch_shapes=[pltpu.VMEM((tm, tn), F32)] if nk > 1 else [],
        compiler_params=_params(("arbitrary", "arbitrary", "arbitrary")),
        name="matmul" + ("_" + act if act else ""),
    )(*([a, a2] if m2 else [a]), w, *[ws for ws, _ in side])
    return (res[0], list(res[1:])) if side else res[0]


def _ret_kernel(q_ref, k_ref, v_ref, g_ref, cos_ref, sin_ref, intra_ref, cross_ref, into_ref, whole_ref,
                s0_ref, *rest, nc, hb, aliased):
    if aliased:
        rest = rest[1:]
    o_ref, sout_ref, *scratch = rest
    c = pl.program_id(2)
    if nc > 1:
        s_scr, = scratch

        @pl.when(c == 0)
        def _():
            s_scr[...] = s0_ref[0]

    cos = cos_ref[...]
    sin = sin_ref[...]
    half = RET_DK // 2

    def rot(x):
        x1 = x[:, :half]
        x2 = x[:, half:]
        return jnp.concatenate([x1 * cos - x2 * sin, x2 * cos + x1 * sin], axis=1)

    for hh in range(hb):
        cols = slice(hh * RET_DK, (hh + 1) * RET_DK)
        qr = rot(q_ref[:, cols]).astype(BF16)
        kr = rot(k_ref[:, cols]) * (RET_DK ** -0.5)
        v = v_ref[:, cols].astype(BF16)
        s = s_scr[hh] if nc > 1 else s0_ref[0, hh]
        sc = _nt(qr, kr.astype(BF16)) * intra_ref[hh]
        o = (jnp.dot(sc.astype(BF16), v, preferred_element_type=F32)
             + jnp.dot(qr, s.astype(BF16), preferred_element_type=F32) * cross_ref[hh])
        s_new = s * whole_ref[hh] + _tn((kr * into_ref[hh]).astype(BF16), v)
        mu = jnp.mean(o, -1, keepdims=True)
        oc = o - mu
        var = jnp.mean(oc * oc, -1, keepdims=True)
        o_ref[:, cols] = (_silu(g_ref[:, cols]) * (oc * lax.rsqrt(var + EPS))).astype(o_ref.dtype)
        if nc > 1:
            s_scr[hh] = s_new

            @pl.when(c == nc - 1)
            def _():
                sout_ref[0, hh] = s_new
        else:
            sout_ref[0, hh] = s_new


def _retention(proj, row_off, b, l, q, hb, pos0, state_stack, j_in, out_stack, j_out, n_stack, out_dtype):
    d = proj.shape[1] // 4
    h = d // RET_DK
    nc = l // q
    rb0 = row_off // q
    nhb = h // hb
    assert row_off % q == 0 and l % q == 0 and h % hb == 0
    half = RET_DK // 2
    inv = jnp.power(ROPE_BASE, -jnp.arange(half, dtype=F32) / half)
    ang = (pos0 + jnp.arange(l, dtype=F32))[:, None] * inv[None, :]
    cos, sin = jnp.cos(ang), jnp.sin(ang)
    idx = jnp.arange(q, dtype=F32)
    diff = idx[:, None] - idx[None, :]
    logg = jnp.log1p(-jnp.exp2(-5.0 - jnp.arange(h, dtype=F32)))
    intra = jnp.where(diff[None] >= 0, jnp.exp(logg[:, None, None] * jnp.maximum(diff, 0.0)[None]), 0.0)
    cross = jnp.exp(logg[:, None] * (idx[None, :] + 1.0))[:, :, None]
    into = jnp.exp(logg[:, None] * (q - 1.0 - idx[None, :]))[:, :, None]
    whole = jnp.exp(logg * q)[:, None, None]

    def part(p):
        return pl.BlockSpec((q, hb * RET_DK), lambda bi, hi, ci: (rb0 + bi * nc + ci, p * nhb + hi))

    tab = pl.BlockSpec((q, half), lambda bi, hi, ci: (ci, 0))
    col = pl.BlockSpec((hb, q, 1), lambda bi, hi, ci: (hi, 0, 0))
    st = pl.BlockSpec((None, 1, hb, RET_DK, RET_DK), lambda bi, hi, ci: (j_in, bi, hi, 0, 0))
    st_out = pl.BlockSpec((None, 1, hb, RET_DK, RET_DK), lambda bi, hi, ci: (j_out, bi, hi, 0, 0))
    aliased = out_stack is not None
    out_index = 1
    in_specs = [part(0), part(1), part(2), part(3), tab, tab,
                pl.BlockSpec((hb, q, q), lambda bi, hi, ci: (hi, 0, 0)), col, col,
                pl.BlockSpec((hb, 1, 1), lambda bi, hi, ci: (hi, 0, 0)), st]
    operands = [proj, proj, proj, proj, cos, sin, intra, cross, into, whole, state_stack]
    if aliased:
        in_specs.append(pl.BlockSpec(memory_space=pl.ANY))
        operands.append(out_stack)
    return pl.pallas_call(
        functools.partial(_ret_kernel, nc=nc, hb=hb, aliased=aliased),
        grid=(b, nhb, nc),
        in_specs=in_specs,
        out_specs=[pl.BlockSpec((q, hb * RET_DK), lambda bi, hi, ci: (bi * nc + ci, hi)), st_out],
        out_shape=[jax.ShapeDtypeStruct((b * l, d), out_dtype),
                   jax.ShapeDtypeStruct((n_stack, b, h, RET_DK, RET_DK), F32)],
        scratch_shapes=[pltpu.VMEM((hb, RET_DK, RET_DK), F32)] if nc > 1 else [],
        input_output_aliases={len(operands) - 1: out_index} if aliased else {},
        compiler_params=_params(("parallel", "parallel", "arbitrary")),
        name="retention",
    )(*operands)


def _conv_kernel(cur_ref, prev_ref, st_ref, w_ref, b_ref, o_ref, *, blocks_per_seq):
    r = pl.program_id(0)
    cur = cur_ref[...]
    first = (r % blocks_per_seq) == 0
    prev = jnp.where(first, st_ref[0], prev_ref[...])
    w = w_ref[...]
    row8 = lax.broadcasted_iota(jnp.int32, prev.shape, 0)
    acc = b_ref[...] + cur * w[SSD_CONV - 1:SSD_CONV, :]
    for tap in range(SSD_CONV - 1):
        sh = SSD_CONV - 1 - tap
        rolled = pltpu.roll(cur, sh, 0)
        head = jnp.where(row8 < sh, pltpu.roll(prev, sh, 0), rolled[:SUBLANES])
        if cur.shape[0] > SUBLANES:
            shifted = jnp.concatenate([head, rolled[SUBLANES:]], axis=0)
        else:
            shifted = head
        acc = acc + shifted * w[tap:tap + 1, :]
    o_ref[...] = _silu(acc)


def _ssd_conv(proj, row_off, b, l, col0, conv_dim, state8, conv_w, conv_b):
    t = _tile(l, 256, SUBLANES)
    tc = _tile(math.gcd(col0, conv_dim), 2048, LANES)
    bps = l // t
    rb0 = row_off // t
    rb8 = row_off // SUBLANES
    t8 = t // SUBLANES
    j0 = col0 // tc
    return pl.pallas_call(
        functools.partial(_conv_kernel, blocks_per_seq=bps),
        grid=(b * bps, conv_dim // tc),
        in_specs=[pl.BlockSpec((t, tc), lambda r, j: (rb0 + r, j0 + j)),
                  pl.BlockSpec((SUBLANES, tc), lambda r, j: (jnp.maximum(rb8 + r * t8 - 1, 0), j0 + j)),
                  pl.BlockSpec((1, SUBLANES, tc), lambda r, j: (r // bps, 0, j)),
                  pl.BlockSpec((SSD_CONV, tc), lambda r, j: (0, j)),
                  pl.BlockSpec((1, tc), lambda r, j: (0, j))],
        out_specs=pl.BlockSpec((t, tc), lambda r, j: (r, j)),
        out_shape=jax.ShapeDtypeStruct((b * l, conv_dim), F32),
        compiler_params=_params(("parallel", "parallel")),
        name="ssd_conv",
    )(proj, proj, state8, conv_w, conv_b.reshape(1, conv_dim))


def _conv8_kernel(cur_ref, st_ref, w_ref, b_ref, o_ref):
    cur = cur_ref[...]
    n = cur.shape[0]
    st = st_ref[...].reshape(n, cur.shape[1])
    w = w_ref[...]
    rmod = jnp.bitwise_and(lax.broadcasted_iota(jnp.int32, cur.shape, 0), SUBLANES - 1)
    acc = b_ref[...] + cur * w[SSD_CONV - 1:SSD_CONV, :]
    for tap in range(SSD_CONV - 1):
        sh = SSD_CONV - 1 - tap
        shifted = jnp.where(rmod < sh, pltpu.roll(st, (n - SUBLANES + sh) % n, 0), pltpu.roll(cur, sh, 0))
        acc = acc + shifted * w[tap:tap + 1, :]
    o_ref[...] = _silu(acc)


def _ssd_conv8(proj, row_off, b, col0, conv_dim, state8, conv_w, conv_b):
    nb = _tile(b, 16, 1)
    t = nb * SUBLANES
    tc = _tile(math.gcd(col0, conv_dim), 2048, LANES)
    assert row_off % t == 0
    rb0 = row_off // t
    j0 = col0 // tc
    return pl.pallas_call(
        _conv8_kernel,
        grid=(b // nb, conv_dim // tc),
        in_specs=[pl.BlockSpec((t, tc), lambda r, j: (rb0 + r, j0 + j)),
                  pl.BlockSpec((nb, SUBLANES, tc), lambda r, j: (r, 0, j)),
                  pl.BlockSpec((SSD_CONV, tc), lambda r, j: (0, j)),
                  pl.BlockSpec((1, tc), lambda r, j: (0, j))],
        out_specs=pl.BlockSpec((t, tc), lambda r, j: (r, j)),
        out_shape=jax.ShapeDtypeStruct((b * SUBLANES, conv_dim), F32),
        compiler_params=_params(("parallel", "parallel")),
        name="ssd_conv8",
    )(proj, state8, conv_w, conv_b.reshape(1, conv_dim))


def _ssd_kernel(xs_ref, bm_ref, cm_ref, z_ref, dtc_ref, dtr_ref, biasc_ref, biasr_ref, alogc_ref, alogr_ref,
                dexp_ref, nw_ref, e_ref, s0_ref, y_ref, sout_ref, *scratch, nc, q, gb):
    c = pl.program_id(2)
    if nc > 1:
        s_scr, = scratch

        @pl.when(c == 0)
        def _():
            s_scr[...] = s0_ref[0]

    ii = lax.broadcasted_iota(jnp.int32, (q, q), 0)
    jj = lax.broadcasted_iota(jnp.int32, (q, q), 1)
    causal = ii >= jj
    lane =lax.broadcasted_iota(jnp.int32, (q, LANES), 1)
    e = e_ref[...]

    def expand(a):
        hi = a.astype(BF16)
        lo = (a - hi.astype(F32)).astype(BF16)
        return (jnp.dot(hi, e, preferred_element_type=F32) + jnp.dot(lo, e, preferred_element_type=F32))

    pending = []
    for gg in range(gb):
        gcols = slice(gg * SSD_GROUP_W, (gg + 1) * SSD_GROUP_W)
        ncols = slice(gg * SSD_STATE, (gg + 1) * SSD_STATE)
        dtc = _softplus(dtc_ref[gg, 0] + biasc_ref[gg])
        lac = dtc * (-jnp.exp(alogc_ref[gg]))
        dtr = _softplus(dtr_ref[gg, 0] + biasr_ref[gg])
        lar = dtr * (-jnp.exp(alogr_ref[gg]))
        cum_c = _cumsum_rows(lac)
        cum_r = _cumsum_cols(lar)
        last = cum_c[q - 1:q, :]

        xs = xs_ref[:, gcols]
        xdt = xs * expand(dtc)
        xdt_b = xdt.astype(BF16)
        bm = bm_ref[:, ncols].astype(BF16)
        cm = cm_ref[:, ncols].astype(BF16)
        cb = _nt(cm, bm)
        pairs = []
        for p in range(SSD_REP // 2):
            x2 = xdt_b[:, p * LANES:(p + 1) * LANES]
            ys = []
            for hh in (2 * p, 2 * p + 1):
                seg = cum_c[:, hh:hh + 1] - cum_r[hh:hh + 1, :]
                dec = jnp.where(causal, jnp.exp(jnp.where(causal, seg, 0.0)), 0.0)
                ys.append(jnp.dot((cb * dec).astype(BF16), x2, preferred_element_type=F32))
            pairs.append(jnp.where(lane < SSD_HEADDIM, ys[0], ys[1]))
        y = jnp.concatenate(pairs, axis=1)

        s = s_scr[gg] if nc > 1 else s0_ref[0, gg]
        y = y + _nt(cm, s.astype(BF16)) * expand(jnp.exp(cum_c))
        wj = jnp.exp(last - cum_c)
        xdtw = (xdt * expand(wj)).astype(BF16)
        y = y + xs * dexp_ref[gg]
        y = y * _silu(z_ref[:, gcols])
        y = y * lax.rsqrt(jnp.mean(y * y, -1, keepdims=True) + EPS)
        pending.append(((y * nw_ref[gg]).astype(y_ref.dtype), xdtw, bm, jnp.exp(last)))

    for gg, (y, xdtw, bm, elast) in enumerate(pending):
        y_ref[:, gg * SSD_GROUP_W:(gg + 1) * SSD_GROUP_W] = y
        s = s_scr[gg] if nc > 1 else s0_ref[0, gg]
        upd = _tn(xdtw, bm)
        dst = s_scr if nc > 1 else sout_ref.at[0]
        for hh in range(SSD_REP):
            rows = slice(hh * SSD_HEADDIM, (hh + 1) * SSD_HEADDIM)
            dst[gg, rows, :] = s[rows, :] * elast[:, hh:hh + 1] + upd[rows, :]

    if nc > 1:
        @pl.when(c == nc - 1)
        def _():
            sout_ref[0] = s_scr[...]


def _ssd_scan(proj, xact, dt_raw, row_off, b, l, q, gb, state, dt_bias, a_log, d_skip, norm_w, out_dtype):
    conv_dim = xact.shape[1]
    heads = dt_raw.shape[1]
    g = heads // SSD_REP
    dinner = heads * SSD_HEADDIM
    gn = g * SSD_STATE
    ngb = g // gb
    assert conv_dim == dinner + 2 * gn and l % q == 0 and row_off % q == 0 and g % gb == 0
    nc = l // q
    nblk = b * nc
    rb0 = row_off // q
    dtc = dt_raw.reshape(nblk, q, g, SSD_REP).transpose(2, 0, 1, 3)
    dtr = dt_raw.reshape(nblk, q, g, SSD_REP).transpose(2, 0, 3, 1)
    vec_c = lambda a: a.reshape(g, 1, SSD_REP)
    vec_r = lambda a: a.reshape(g, SSD_REP, 1)
    dexp = jnp.repeat(d_skip, SSD_HEADDIM).reshape(g, 1, SSD_GROUP_W)
    nw = norm_w.reshape(g, 1, SSD_GROUP_W)
    e = (jnp.arange(SSD_GROUP_W)[None, :] // SSD_HEADDIM == jnp.arange(SSD_REP)[:, None]).astype(BF16)
    st = state.reshape(b, g, SSD_GROUP_W, SSD_STATE)
    nb0 = dinner // (gb * SSD_STATE)

    blk = lambda bi, gi, ci: bi * nc + ci
    vc = pl.BlockSpec((gb, 1, SSD_REP), lambda bi, gi, ci: (gi, 0, 0))
    vr = pl.BlockSpec((gb, SSD_REP, 1), lambda bi, gi, ci: (gi, 0, 0))
    vg = pl.BlockSpec((gb, 1, SSD_GROUP_W), lambda bi, gi, ci: (gi, 0, 0))
    stspec = pl.BlockSpec((1, gb, SSD_GROUP_W, SSD_STATE), lambda bi, gi, ci: (bi, gi, 0, 0))
    y, s_out = pl.pallas_call(
        functools.partial(_ssd_kernel, nc=nc, q=q, gb=gb),
        grid=(b, ngb, nc),
        in_specs=[pl.BlockSpec((q, gb * SSD_GROUP_W), lambda bi, gi, ci: (blk(bi, gi, ci), gi)),
                  pl.BlockSpec((q, gb * SSD_STATE), lambda bi, gi, ci: (blk(bi, gi, ci), nb0 + gi)),
                  pl.BlockSpec((q, gb * SSD_STATE), lambda bi, gi, ci: (blk(bi, gi, ci), nb0 + ngb + gi)),
                  pl.BlockSpec((q, gb * SSD_GROUP_W), lambda bi, gi, ci: (rb0 + blk(bi, gi, ci), gi)),
                  pl.BlockSpec((gb, 1, q, SSD_REP), lambda bi, gi, ci: (gi, blk(bi, gi, ci), 0, 0)),
                  pl.BlockSpec((gb, 1, SSD_REP, q), lambda bi, gi, ci: (gi, blk(bi, gi, ci), 0, 0)),
                  vc, vr, vc, vr, vg, vg,
                  pl.BlockSpec((SSD_REP, SSD_GROUP_W), lambda bi, gi, ci: (0, 0)),
                  stspec],
        out_specs=[pl.BlockSpec((q, gb * SSD_GROUP_W), lambda bi, gi, ci: (blk(bi, gi, ci), gi)), stspec],
        out_shape=[jax.ShapeDtypeStruct((b * l, dinner), out_dtype),
                   jax.ShapeDtypeStruct(st.shape, F32)],
        scratch_shapes=[pltpu.VMEM((gb, SSD_GROUP_W, SSD_STATE), F32)] if nc > 1 else [],
        compiler_params=_params(("parallel", "parallel", "arbitrary")),
        name="ssd_scan",
    )(xact, xact, xact, proj, dtc, dtr, vec_c(dt_bias), vec_r(dt_bias), vec_c(a_log), vec_r(a_log),
      dexp, nw, e, st)
    return y, s_out.reshape(state.shape)


def _mlstm_kernel(q_ref, k_ref, v_ref, og_ref, gc_ref, gr_ref, bc_ref, br_ref, c0_ref, n0_ref, m0_ref,
                  h_ref, cout_ref, nout_ref, mout_ref, *scratch, nc, q, hb):
    c = pl.program_id(2)
    if nc > 1:
        c_scr, n_scr, m_scr = scratch

        @pl.when(c == 0)
        def _():
            c_scr[...] = c0_ref[0]
            n_scr[...] = n0_ref[0]
            m_scr[...] = m0_ref[0]

    ii = lax.broadcasted_iota(jnp.int32, (q, q), 0)
    jj = lax.broadcasted_iota(jnp.int32, (q, q), 1)
    causal = ii >= jj

    pending = []
    for hh in range(hb):
        qcols = slice(hh * ML_DQK, (hh + 1) * ML_DQK)
        vcols = slice(hh * ML_DV, (hh + 1) * ML_DV)
        gcol = gc_ref[hh, 0] + bc_ref[hh]
        grow = gr_ref[hh, 0] + br_ref[hh]
        ig_c = gcol[:, 0:1]
        ig_r = grow[0:1, :]
        lf_c = _log_sigmoid(gcol)
        lf_r = _log_sigmoid(grow)
        bcum_c = _cumsum_rows(lf_c)[:, 1:2]
        bcum_r = _cumsum_cols(lf_r)[1:2, :]

        if nc > 1:
            cs, ns, m_prev = c_scr[hh], n_scr[hh], m_scr[hh]
        else:
            cs, ns, m_prev = c0_ref[0, hh], n0_ref[0, hh], m0_ref[0, hh]
        logw = jnp.where(causal, bcum_c - bcum_r + ig_r, -jnp.inf)
        logs = bcum_c + m_prev
        mt = jnp.maximum(logs, jnp.max(logw, axis=1, keepdims=True))
        qf = q_ref[:, qcols]
        kf = k_ref[:, qcols] * (ML_DQK ** -0.5)
        qb = qf.astype(BF16)
        v = v_ref[:, vcols].astype(BF16)
        qk = _nt(qb, kf.astype(BF16)) * jnp.exp(logw - mt)
        ws = jnp.exp(logs - mt)
        num = (jnp.dot(qk.astype(BF16), v, preferred_element_type=F32)
               + jnp.dot(qb, cs.astype(BF16), preferred_element_type=F32) * ws)
        den = jnp.sum(qk, axis=1, keepdims=True) + jnp.sum(qf * ns, axis=1, keepdims=True) * ws
        hc = num / jnp.maximum(jnp.abs(den), jnp.exp(-mt))
        h_out = (jax.nn.sigmoid(og_ref[:, vcols]) * hc).astype(h_ref.dtype)

        m_new = mt[q - 1:q, :]
        blast = bcum_c[q - 1:q, :]
        wl = jnp.exp(blast - bcum_c + ig_c - m_new)
        sl = jnp.exp(blast + m_prev - m_new)
        kw = kf * wl
        n_new = ns * sl + jnp.sum(kw, axis=0, keepdims=True)
        pending.append((h_out, kw.astype(BF16), v, sl, n_new, m_new))

    for hh, (h_out, kw, v, sl, n_new, m_new) in enumerate(pending):
        h_ref[:, hh * ML_DV:(hh + 1) * ML_DV] = h_out
        cs = c_scr[hh] if nc > 1 else c0_ref[0, hh]
        c_new = cs * sl + _tn(kw, v)
        if nc > 1:
            c_scr[hh] = c_new
            n_scr[hh] = n_new
            m_scr[hh] = m_new

            @pl.when(c == nc - 1)
            def _():
                cout_ref[0, hh] = c_new
                nout_ref[0, hh] = n_new
                mout_ref[0, hh] = m_new
        else:
            cout_ref[0, hh] = c_new
            nout_ref[0, hh] = n_new
            mout_ref[0, hh] = m_new


def _mlstm(proj, graw, row_off, b, l, q, hb, c_state, n_state, m_state, b_gates, out_dtype):
    h = graw.shape[1] // 2
    d = h * ML_DV
    nc = l // q
    nblk = b * nc
    rb0 = row_off // q
    nhb = h // hb
    assert row_off % q == 0 and l % q == 0 and h % hb == 0
    g4 = graw.reshape(nblk, q, 2, h)
    gc = g4.transpose(3, 0, 1, 2)
    gr = g4.transpose(3, 0, 2, 1)
    bc = b_gates.reshape(2, h).T.reshape(h, 1, 2)
    br = b_gates.reshape(2, h).T.reshape(h, 2, 1)
    kb0 = nhb
    vb0 = (2 * h * ML_DQK) // (hb * ML_DV)
    ob0 = vb0 + nhb

    blk = lambda bi, hi, ci: bi * nc + ci
    cst = pl.BlockSpec((1, hb, ML_DQK, ML_DV), lambda bi, hi, ci: (bi, hi, 0, 0))
    nst = pl.BlockSpec((1, hb, 1, ML_DQK), lambda bi, hi, ci: (bi, hi, 0, 0))
    mst = pl.BlockSpec((1, hb, 1, 1), lambda bi, hi, ci: (bi, hi, 0, 0))
    scratch = []
    if nc > 1:
        scratch = [pltpu.VMEM((hb, ML_DQK, ML_DV), F32), pltpu.VMEM((hb, 1, ML_DQK), F32),
                   pltpu.VMEM((hb, 1, 1), F32)]
    hs, c_out, n_out, m_out = pl.pallas_call(
        functools.partial(_mlstm_kernel, nc=nc, q=q, hb=hb),
        grid=(b, nhb, nc),
        in_specs=[pl.BlockSpec((q, hb * ML_DQK), lambda bi, hi, ci: (rb0 + blk(bi, hi, ci), hi)),
                  pl.BlockSpec((q, hb * ML_DQK), lambda bi, hi, ci: (rb0 + blk(bi, hi, ci), kb0 + hi)),
                  pl.BlockSpec((q, hb * ML_DV), lambda bi, hi, ci: (rb0 + blk(bi, hi, ci), vb0 + hi)),
                  pl.BlockSpec((q, hb * ML_DV), lambda bi, hi, ci: (rb0 + blk(bi, hi, ci), ob0 + hi)),
                  pl.BlockSpec((hb, 1, q, 2), lambda bi, hi, ci: (hi, blk(bi, hi, ci), 0, 0)),
                  pl.BlockSpec((hb, 1, 2, q), lambda bi, hi, ci: (hi, blk(bi, hi, ci), 0, 0)),
                  pl.BlockSpec((hb, 1, 2), lambda bi, hi, ci: (hi, 0, 0)),
                  pl.BlockSpec((hb, 2, 1), lambda bi, hi, ci: (hi, 0, 0)),
                  cst, nst, mst],
        out_specs=[pl.BlockSpec((q, hb * ML_DV), lambda bi, hi, ci: (blk(bi, hi, ci), hi)), cst, nst, mst],
        out_shape=[jax.ShapeDtypeStruct((b * l, d), out_dtype),
                   jax.ShapeDtypeStruct((b, h, ML_DQK, ML_DV), F32),
                   jax.ShapeDtypeStruct((b, h, 1, ML_DQK), F32),
                   jax.ShapeDtypeStruct((b, h, 1, 1), F32)],
        scratch_shapes=scratch,
        compiler_params=_params(("parallel", "parallel", "arbitrary")),
        name="mlstm",
    )(proj, proj, proj, proj, gc, gr, bc, br, c_state, n_state.reshape(b, h, 1, ML_DQK),
      m_state.reshape(b, h, 1, 1))
    return hs, c_out, n_out.reshape(b, h, ML_DQK), m_out.reshape(b, h)


def _chunk(l, pref):
    return _tile(l, pref, SUBLANES)


def _pad_cols(w):
    return jnp.pad(w, ((0, 0), (0, -w.shape[1] % LANES)))


def kernel(x_prompt, x_sample, state_ret, state_ssm, state_conv, state_mlstm_C, state_mlstm_n, state_mlstm_m, norm_mix_pre, norm_mix_post, norm_ffn_pre, norm_ffn_post, ret_w_in, ret_w_out, ssd_w_in, ssd_conv_w, ssd_conv_b, ssd_dt_bias, ssd_a_log, ssd_d, ssd_norm_w, ssd_w_out, ml_w_in, ml_b_gates, ml_w_out, ffn_w_up, ffn_w_down):
    bp, lp, d = x_prompt.shape
    bs, ls, _ = x_sample.shape
    depth = norm_mix_pre.shape[0]
    mp, ms = bp * lp, bs * ls
    assert lp >= SSD_CONV - 1 and ls >= SSD_CONV - 1
    dinner = ssd_w_out.shape[1]
    ssd_heads = dinner // SSD_HEADDIM
    ssd_groups = ssd_heads // SSD_REP
    conv_dim = ssd_conv_w.shape[2]
    ml_heads = d // ML_DV
    ml_qk = ml_heads * ML_DQK
    ret_heads = d // RET_DK
    n_ret = state_ret.shape[0]
    qs = _chunk(ls, 256)
    dt = x_prompt.dtype

    x_parts = (x_prompt.reshape(mp, d), x_sample.reshape(ms, d))
    hn = _prenorm2(*x_parts, norm_mix_pre[0])
    new = {k: ([], []) for k in ("ssm", "conv", "c", "n", "m")}
    ret_p = ret_s = None
    zero_ret = jnp.zeros((1, bp, ret_heads, RET_DK, RET_DK), F32)

    in_stacks = (ret_w_in, ssd_w_in, ml_w_in)
    out_stacks = (ret_w_out, ssd_w_out, ml_w_out)
    w_in = _to_bf16(in_stacks[0], 0)

    for i in range(depth):
        j = i // 3
        kind = i % 3
        side_in = ((ffn_w_up, i), (out_stacks[kind], j))
        if kind == 0:
            proj, (w_up, w_out) = _matmul(hn, w_in, side=side_in)
            op, ret_p = _retention(proj, 0, bp, lp, _chunk(lp, 256), _tile(ret_heads, 4, 1), 0.0,
                                   zero_ret, 0, ret_p, j, n_ret, BF16)
            os_, ret_s = _retention(proj, mp, bs, ls, qs, ret_heads, float(PAST_LEN),
                                    state_ret, j, ret_s, j, n_ret, BF16)
        elif kind == 1:
            main = dinner + conv_dim
            proj, (w_up, w_out) = _matmul(hn, w_in, n_cols=main, side=side_in)
            dt_raw = _matmul(hn, _pad_cols(w_in[:, main:]))[:, :ssd_heads]
            outs = []
            for g_i, (off, b, l, q, gb, st_ssm, st_conv) in enumerate((
                    (0, bp, lp, _chunk(lp, 128), _tile(ssd_groups, 2, 1), jnp.zeros((bp,) + state_ssm.shape[2:], F32),
                     jnp.zeros((bp, SSD_CONV - 1, conv_dim), F32)),
                    (mp, bs, ls, qs, ssd_groups, state_ssm[j], state_conv[j]))):
                st8 = jnp.pad(st_conv, ((0, 0), (SUBLANES - (SSD_CONV - 1), 0), (0, 0)))
                if l == SUBLANES:
                    xact = _ssd_conv8(proj, off, b, dinner, conv_dim, st8, ssd_conv_w[j], ssd_conv_b[j])
                else:
                    xact = _ssd_conv(proj, off, b, l, dinner, conv_dim, st8, ssd_conv_w[j], ssd_conv_b[j])
                y, s_new = _ssd_scan(proj, xact, dt_raw[off:off + b * l], off, b, l, q, gb, st_ssm,
                                     ssd_dt_bias[j], ssd_a_log[j], ssd_d[j], ssd_norm_w[j], BF16)
                outs.append(y)
                new["ssm"][g_i].append(s_new)
                tails = [lax.slice(proj, (off + (bi + 1) * l - (SSD_CONV - 1), dinner), (off + (bi + 1) * l, main))
                         for bi in range(b)] if b <= 8 else None
                if tails is None:
                    tails = lax.slice(proj, (off, dinner), (off + b * l, main)).reshape(b, l, conv_dim)
                    new["conv"][g_i].append(tails[:, l - (SSD_CONV - 1):])
                else:
                    new["conv"][g_i].append(jnp.stack(tails))
            op, os_ = outs
        else:
            main = 2 * ml_qk + 2 * d
            proj, (w_up, w_out) = _matmul(hn, w_in, n_cols=main, side=side_in)
            graw = _matmul(hn, _pad_cols(w_in[:, main:]))[:, :2 * ml_heads]
            op, cp, np_, mp_ = _mlstm(proj, graw[:mp], 0, bp, lp, _chunk(lp, 128), _tile(ml_heads, 2, 1),
                                      jnp.zeros((bp,) + state_mlstm_C.shape[2:], F32),
                                      jnp.zeros((bp,) + state_mlstm_n.shape[2:], F32),
                                      jnp.zeros((bp,) + state_mlstm_m.shape[2:], F32), ml_b_gates[j], BF16)
            os_, cs_, ns_, ms_ = _mlstm(proj, graw[mp:], mp, bs, ls, qs, ml_heads, state_mlstm_C[j],
                                        state_mlstm_n[j], state_mlstm_m[j], ml_b_gates[j], BF16)
            for key, vp, vs in (("c", cp, cs_), ("n", np_, ns_), ("m", mp_, ms_)):
                new[key][0].append(vp)
                new[key][1].append(vs)

        out = _matmul(op, w_out, a2=os_, out_dtype=BF16)
        x, hn = _post_residual_prenorm(out, x_parts, norm_mix_post[i], norm_ffn_pre[i])
        up, (w_down,) = _matmul(hn, w_up, act="relu2", out_dtype=BF16, side=((ffn_w_down, i),))
        if i + 1 < depth:
            nxt = ((in_stacks[(i + 1) % 3], (i + 1) // 3),)
            f, (w_in,) = _matmul(up, w_down, out_dtype=BF16, side=nxt)
            x, hn = _post_residual_prenorm(f, (x,), norm_ffn_post[i], norm_mix_pre[i + 1])
            x_parts = (x,)
        else:
            f = _matmul(up, w_down, out_dtype=BF16)
            y_prompt, _ = _post_residual_prenorm(f, (x,), norm_ffn_post[i], None, 0, mp)
            y_sample, _ = _post_residual_prenorm(f, (x,), norm_ffn_post[i], None, mp, ms)

    st = lambda key, g_i: jnp.stack(new[key][g_i]).astype(dt)
    return (y_prompt.reshape(bp, lp, d), y_sample.reshape(bs, ls, d), ret_p.astype(dt), ret_s.astype(dt),
            st("ssm", 0), st("ssm", 1), st("conv", 0), st("conv", 1), st("c", 0), st("c", 1),
            st("n", 0), st("n", 1), st("m", 0), st("m", 1))
```

```python
import functools
import math

import jax
import jax.numpy as jnp
from jax import lax
from jax.experimental import pallas as pl
from jax.experimental.pallas import tpu as pltpu

F32 = jnp.float32
BF16 = jnp.bfloat16
HIGHEST = lax.Precision.HIGHEST

EPS = 1e-6
PAST_LEN = 16384
ROPE_BASE = 10000.0
RET_DK = 256
SSD_HEADDIM = 64
SSD_STATE = 128
SSD_REP = 16
SSD_CONV = 4
SSD_GROUP_W = SSD_REP * SSD_HEADDIM
ML_DQK = 256
ML_DV = 512

LANES = 128
SUBLANES = 8
BF16_ROWS = 16
VMEM_LIMIT_BYTES = 58 * 1024 * 1024
CAST_BLOCK_BYTES = 8 * 1024 * 1024


def _params(sem):
    return pltpu.CompilerParams(dimension_semantics=sem, vmem_limit_bytes=VMEM_LIMIT_BYTES)


def _tile(dim, pref, mult):
    best = None
    t = mult
    while t <= min(dim, pref):
        if dim % t == 0:
            best = t
        t += mult
    assert best is not None, (dim, pref, mult)
    return best


def _nt(a, b):
    return lax.dot_general(a, b, (((1,), (1,)), ((), ())), preferred_element_type=F32)


def _tn(a, b):
    return lax.dot_general(a, b, (((0,), (0,)), ((), ())), preferred_element_type=F32)


def _silu(x):
    return x * jax.nn.sigmoid(x)


def _softplus(x):
    return jnp.maximum(x, 0.0) + jnp.log1p(jnp.exp(-jnp.abs(x)))


def _log_sigmoid(x):
    return -_softplus(-x)


SHORT_CHUNK = 16


def _cumsum_rows(x):
    q = x.shape[0]
    if q <= SHORT_CHUNK:
        row = lax.broadcasted_iota(jnp.int32, x.shape, 0)
        acc = jnp.zeros_like(x)
        for j in range(q):
            acc = acc + jnp.where(row >= j, x[j:j + 1, :], 0.0)
        return acc
    tri = (lax.broadcasted_iota(jnp.int32, (q, q), 0) >= lax.broadcasted_iota(jnp.int32, (q, q), 1)).astype(F32)
    return jnp.dot(tri, x, preferred_element_type=F32, precision=HIGHEST)


def _cumsum_cols(x):
    q = x.shape[1]
    if q <= SHORT_CHUNK:
        lane = lax.broadcasted_iota(jnp.int32, x.shape, 1)
        acc = jnp.zeros_like(x)
        for j in range(q):
            acc = acc + jnp.where(lane >= j, x[:, j:j + 1], 0.0)
        return acc
    triu = (lax.broadcasted_iota(jnp.int32, (q, q), 1) >= lax.broadcasted_iota(jnp.int32, (q, q), 0)).astype(F32)
    return jnp.dot(x, triu, preferred_element_type=F32, precision=HIGHEST)


def _cast_kernel(w_ref, o_ref):
    o_ref[...] = w_ref[...].astype(o_ref.dtype)


def _to_bf16(w_stack, j):
    _, r, c = w_stack.shape
    tr = _tile(r, max(BF16_ROWS, CAST_BLOCK_BYTES // (4 * c)), BF16_ROWS)
    return pl.pallas_call(
        _cast_kernel,
        grid=(r // tr,),
        in_specs=[pl.BlockSpec((None, tr, c), lambda i: (j, i, 0))],
        out_specs=pl.BlockSpec((tr, c), lambda i: (i, 0)),
        out_shape=jax.ShapeDtypeStruct((r, c), BF16),
        compiler_params=_params(("parallel",)),
        name="cast_bf16",
    )(w_stack)


def _rms(x, w):
    return x * lax.rsqrt(jnp.mean(x * x, -1, keepdims=True) + EPS) * w


def _two_part_specs(tm, d, n1):
    return [pl.BlockSpec((tm, d), lambda i: (jnp.minimum(i, n1 - 1), 0)),
            pl.BlockSpec((tm, d), lambda i: (jnp.maximum(i - n1, 0), 0))]


def _prenorm2_kernel(x1_ref, x2_ref, w_ref, h_ref, *, n1):
    x = jnp.where(pl.program_id(0) < n1, x1_ref[...], x2_ref[...])
    h_ref[...] = _rms(x, w_ref[...]).astype(h_ref.dtype)


def _prenorm2(x1, x2, w):
    (m1, d), m2 = x1.shape, x2.shape[0]
    tm = _tile(math.gcd(m1, m2), 512, SUBLANES)
    n1 = m1 // tm
    return pl.pallas_call(
        functools.partial(_prenorm2_kernel, n1=n1),
        grid=((m1 + m2) // tm,),
        in_specs=_two_part_specs(tm, d, n1) + [pl.BlockSpec((1, d), lambda i: (0, 0))],
        out_specs=pl.BlockSpec((tm, d), lambda i: (i, 0)),
        out_shape=jax.ShapeDtypeStruct((m1 + m2, d), BF16),
        compiler_params=_params(("parallel",)),
        name="prenorm2",
    )(x1, x2, w.reshape(1, d))


def _post_kernel(*refs, n1, with_next):
    refs = list(refs)
    o_ref = refs.pop(0)
    if n1 is None:
        x = refs.pop(0)[...]
    else:
        x1_ref, x2_ref = refs.pop(0), refs.pop(0)
        x = jnp.where(pl.program_id(0) < n1, x1_ref[...], x2_ref[...])
    wpost_ref = refs.pop(0)
    xn = x + _rms(o_ref[...].astype(F32), wpost_ref[...])
    if with_next:
        wpre_ref, xn_ref, hn_ref = refs
        hn_ref[...] = _rms(xn, wpre_ref[...]).astype(hn_ref.dtype)
    else:
        xn_ref, = refs
    xn_ref[...] = xn


def _post_residual_prenorm(o, x_parts, w_post, w_pre_next, row_off=0, rows=None):
    d = o.shape[1]
    rows = o.shape[0] if rows is None else rows
    two = len(x_parts) == 2
    tm = _tile(math.gcd(math.gcd(rows, row_off), x_parts[0].shape[0]), 256, SUBLANES)
    r0 = row_off // tm
    row = pl.BlockSpec((tm, d), lambda i: (r0 + i, 0))
    out_row = pl.BlockSpec((tm, d), lambda i: (i, 0))
    vec = pl.BlockSpec((1, d), lambda i: (0, 0))
    n1 = x_parts[0].shape[0] // tm if two else None
    assert not (two and row_off)
    with_next = w_pre_next is not None
    xs = jax.ShapeDtypeStruct((rows, d), F32)
    res = pl.pallas_call(
        functools.partial(_post_kernel, n1=n1, with_next=with_next),
        grid=(rows // tm,),
        in_specs=[row] + (_two_part_specs(tm, d, n1) if two else [row]) + [vec] + ([vec] if with_next else []),
        out_specs=[out_row, out_row] if with_next else [out_row],
        out_shape=[xs, jax.ShapeDtypeStruct((rows, d), BF16)] if with_next else [xs],
        compiler_params=_params(("parallel",)),
        name="post_residual_prenorm",
    )(o, *x_parts, w_post.reshape(1, d), *([w_pre_next.reshape(1, d)] if with_next else []))
    return (res[0], res[1]) if with_next else (res[0], None)


def _mm_kernel(*refs, nk, act, n1, n_side):
    refs = list(refs)
    if n1 is None:
        a_ref = refs.pop(0)
        load_a = lambda: a_ref[...]
    else:
        a1_ref, a2_ref = refs.pop(0), refs.pop(0)
        load_a = lambda: jnp.where(pl.program_id(0) < n1, a1_ref[...], a2_ref[...])
    w_ref = refs.pop(0)
    side_in = [refs.pop(0) for _ in range(n_side)]
    o_ref = refs.pop(0)
    side_out = [refs.pop(0) for _ in range(n_side)]
    scratch = refs

    for src, dst in zip(side_in, side_out):
        dst[...] = src[...].astype(dst.dtype)

    def finish(r):
        if act == "relu2":
            r = jnp.square(jnp.maximum(r, 0.0))
        o_ref[...] = r.astype(o_ref.dtype)

    def prod():
        return jnp.dot(load_a(), w_ref[...], preferred_element_type=F32)

    if nk == 1:
        finish(prod())
        return
    acc_ref, = scratch
    k = pl.program_id(2)

    @pl.when(k == 0)
    def _():
        acc_ref[...] = prod()

    @pl.when(jnp.logical_and(k > 0, k < nk - 1))
    def _():
        acc_ref[...] += prod()

    @pl.when(k == nk - 1)
    def _():
        finish(acc_ref[...] + prod())


def _matmul(a, w, n_cols=None, act=None, out_dtype=F32, a2=None, side=()):
    m1, kdim = a.shape
    m2 = 0 if a2 is None else a2.shape[0]
    m = m1 + m2
    n_cols = w.shape[1] if n_cols is None else n_cols
    tm = _tile(math.gcd(m1, m2), 512, SUBLANES) if m2 else _tile(m1, 1024, SUBLANES)
    tk = _tile(kdim, 4096, LANES)
    nk = kdim // tk
    out_bytes = jnp.dtype(out_dtype).itemsize

    def side_rows(r, steps):
        return min(t for t in range(BF16_ROWS, r + 1, BF16_ROWS) if r % t == 0 and r // t <= steps)

    def vmem_bytes(tn):
        steps = (m // tm) * (n_cols // tn) * nk
        blocks = (2 if m2 else 1) * tm * tk * 2 + tk * tn * 2 + tm * tn * out_bytes
        blocks += sum(side_rows(ws.shape[1], steps) * ws.shape[2] * (4 + 2) for ws, _ in side)
        return 2 * blocks + (tm * tn * 4 if nk > 1 else 0) + 2 * tm * tn * 4

    tn = next(t for t in (_tile(n_cols, p, LANES) for p in (1024, 512, 256, 128))
              if vmem_bytes(t) <= VMEM_LIMIT_BYTES or t == _tile(n_cols, 128, LANES))
    nj = n_cols // tn
    steps = (m // tm) * nj * nk
    n1 = m1 // tm if m2 else None
    if m2:
        a_specs = [pl.BlockSpec((tm, tk), lambda i, j, k: (jnp.minimum(i, n1 - 1), k)),
                   pl.BlockSpec((tm, tk), lambda i, j, k: (jnp.maximum(i - n1, 0), k))]
    else:
        a_specs = [pl.BlockSpec((tm, tk), lambda i, j, k: (i, k))]

    side_in_specs, side_out_specs, side_shapes = [], [], []
    for w_stack, js in side:
        _, r, c = w_stack.shape
        tr = side_rows(r, steps)
        last = r // tr - 1

        def block(i, j, k, last=last):
            return jnp.minimum((i * nj + j) * nk + k, last)

        side_in_specs.append(pl.BlockSpec((None, tr, c), lambda i, j, k, js=js, block=block: (js, block(i, j, k), 0)))
        side_out_specs.append(pl.BlockSpec((tr, c), lambda i, j, k, block=block: (block(i, j, k), 0)))
        side_shapes.append(jax.ShapeDtypeStruct((r, c), BF16))

    res = pl.pallas_call(
        functools.partial(_mm_kernel, nk=nk, act=act, n1=n1, n_side=len(side)),
        grid=(m // tm, nj, nk),
        in_specs=a_specs + [pl.BlockSpec((tk, tn), lambda i, j, k: (k, j))] + side_in_specs,
        out_specs=[pl.BlockSpec((tm, tn), lambda i, j, k: (i, j))] + side_out_specs,
        out_shape=[jax.ShapeDtypeStruct((m, n_cols), out_dtype)] + side_shapes,
        scratch_shapes=[pltpu.VMEM((tm, tn), F32)] if nk > 1 else [],
        compiler_params=_params(("arbitrary", "arbitrary", "arbitrary")),
        name="matmul" + ("_" + act if act else ""),
    )(*([a, a2] if m2 else [a]), w, *[ws for ws, _ in side])
    return (res[0], list(res[1:])) if side else res[0]


def _ret_kernel(q_ref, k_ref, v_ref, g_ref, cos_ref, sin_ref, intra_ref, cross_ref, into_ref, whole_ref,
                s0_ref, *rest, nc, hb, emit_state, n_deferred):
    rest = list(rest)
    deferred = [(rest.pop(0), rest.pop(0), rest.pop(0)) for _ in range(n_deferred)]
    o_ref = rest.pop(0)
    sout_ref = rest.pop(0) if emit_state else None
    scratch = rest
    c = pl.program_id(2)
    if nc > 1:
        s_scr, = scratch

        @pl.when(c == 0)
        def _():
            s_scr[...] = s0_ref[0]

    cos = cos_ref[...]
    sin = sin_ref[...]
    half = RET_DK // 2

    def rot(x):
        x1 = x[:, :half]
        x2 = x[:, half:]
        return jnp.concatenate([x1 * cos - x2 * sin, x2 * cos + x1 * sin], axis=1)

    for hh in range(hb):
        cols = slice(hh * RET_DK, (hh + 1) * RET_DK)
        qr = rot(q_ref[:, cols]).astype(BF16)
        kr = rot(k_ref[:, cols]) * (RET_DK ** -0.5)
        v = v_ref[:, cols].astype(BF16)
        s = s_scr[hh] if nc > 1 else s0_ref[0, hh]
        sc = _nt(qr, kr.astype(BF16)) * intra_ref[hh]
        o = (jnp.dot(sc.astype(BF16), v, preferred_element_type=F32)
             + jnp.dot(qr, s.astype(BF16), preferred_element_type=F32) * cross_ref[hh])
        s_new = s * whole_ref[hh] + _tn((kr * into_ref[hh]).astype(BF16), v)
        mu = jnp.mean(o, -1, keepdims=True)
        oc = o - mu
        var = jnp.mean(oc * oc, -1, keepdims=True)
        o_ref[:, cols] = (_silu(g_ref[:, cols]) * (oc * lax.rsqrt(var + EPS))).astype(o_ref.dtype)
        if nc > 1:
            s_scr[hh] = s_new
        if emit_state:
            if nc > 1:
                @pl.when(c == nc - 1)
                def _():
                    sout_ref[n_deferred, 0, hh] = s_new
            else:
                sout_ref[n_deferred, 0, hh] = s_new
        for slot, (kd_ref, vd_ref, sd_ref) in enumerate(deferred):
            kd = (rot(kd_ref[:, cols]) * (RET_DK ** -0.5) * into_ref[hh]).astype(BF16)
            sout_ref[slot, 0, hh] = sd_ref[0, hh] * whole_ref[hh] + _tn(kd, vd_ref[:, cols].astype(BF16))


def _retention(proj, row_off, b, l, q, hb, pos0, state_stack, j_in, out_dtype, emit_state=True, deferred=()):
    d = proj.shape[1] // 4
    h = d // RET_DK
    nc = l // q
    rb0 = row_off // q
    nhb = h // hb
    assert row_off % q == 0 and l % q == 0 and h % hb == 0
    half = RET_DK // 2
    inv = jnp.power(ROPE_BASE, -jnp.arange(half, dtype=F32) / half)
    ang = (pos0 + jnp.arange(l, dtype=F32))[:, None] * inv[None, :]
    cos, sin = jnp.cos(ang), jnp.sin(ang)
    idx = jnp.arange(q, dtype=F32)
    diff = idx[:, None] - idx[None, :]
    logg = jnp.log1p(-jnp.exp2(-5.0 - jnp.arange(h, dtype=F32)))
    intra = jnp.where(diff[None] >= 0, jnp.exp(logg[:, None, None] * jnp.maximum(diff, 0.0)[None]), 0.0)
    cross = jnp.exp(logg[:, None] * (idx[None, :] + 1.0))[:, :, None]
    into = jnp.exp(logg[:, None] * (q - 1.0 - idx[None, :]))[:, :, None]
    whole = jnp.exp(logg * q)[:, None, None]

    def part(p):
        return pl.BlockSpec((q, hb * RET_DK), lambda bi, hi, ci: (rb0 + bi * nc + ci, p * nhb + hi))

    tab = pl.BlockSpec((q, half), lambda bi, hi, ci: (ci, 0))
    col = pl.BlockSpec((hb, q, 1), lambda bi, hi, ci: (hi, 0, 0))
    def st(jd):
        return pl.BlockSpec((None, 1, hb, RET_DK, RET_DK), lambda bi, hi, ci: (jd, bi, hi, 0, 0))

    assert not deferred or (nc == 1 and emit_state)
    n_slots = len(deferred) + 1
    in_specs = [part(0), part(1), part(2), part(3), tab, tab,
                pl.BlockSpec((hb, q, q), lambda bi, hi, ci: (hi, 0, 0)), col, col,
                pl.BlockSpec((hb, 1, 1), lambda bi, hi, ci: (hi, 0, 0)), st(j_in)]
    operands = [proj, proj, proj, proj, cos, sin, intra, cross, into, whole, state_stack]
    for proj_d, jd in deferred:
        in_specs += [part(1), part(2), st(jd)]
        operands += [proj_d, proj_d, state_stack]
    out_specs = [pl.BlockSpec((q, hb * RET_DK), lambda bi, hi, ci: (bi * nc + ci, hi))]
    out_shape = [jax.ShapeDtypeStruct((b * l, d), out_dtype)]
    if emit_state:
        out_specs.append(pl.BlockSpec((n_slots, 1, hb, RET_DK, RET_DK), lambda bi, hi, ci: (0, bi, hi, 0, 0)))
        out_shape.append(jax.ShapeDtypeStruct((n_slots, b, h, RET_DK, RET_DK), F32))
    res = pl.pallas_call(
        functools.partial(_ret_kernel, nc=nc, hb=hb, emit_state=emit_state, n_deferred=len(deferred)),
        grid=(b, nhb, nc),
        in_specs=in_specs,
        out_specs=out_specs,
        out_shape=out_shape,
        scratch_shapes=[pltpu.VMEM((hb, RET_DK, RET_DK), F32)] if nc > 1 else [],
        compiler_params=_params(("parallel", "parallel", "arbitrary")),
        name="retention",
    )(*operands)
    return (res[0], res[1]) if emit_state else (res[0], None)


def _conv_kernel(cur_ref, prev_ref, st_ref, w_ref, b_ref, o_ref, *, blocks_per_seq):
    r = pl.program_id(0)
    cur = cur_ref[...]
    first = (r % blocks_per_seq) == 0
    prev = jnp.where(first, st_ref[0], prev_ref[...])
    w = w_ref[...]
    row8 = lax.broadcasted_iota(jnp.int32, prev.shape, 0)
    acc = b_ref[...] + cur * w[SSD_CONV - 1:SSD_CONV, :]
    for tap in range(SSD_CONV - 1):
        sh = SSD_CONV - 1 - tap
        rolled = pltpu.roll(cur, sh, 0)
        head = jnp.where(row8 < sh, pltpu.roll(prev, sh, 0), rolled[:SUBLANES])
        if cur.shape[0] > SUBLANES:
            shifted = jnp.concatenate([head, rolled[SUBLANES:]], axis=0)
        else:
            shifted = head
        acc = acc + shifted * w[tap:tap + 1, :]
    o_ref[...] = _silu(acc)


def _ssd_conv(proj, row_off, b, l, col0, conv_dim, state8, conv_w, conv_b):
    t = _tile(l, 256, SUBLANES)
    tc = _tile(math.gcd(col0, conv_dim), 2048, LANES)
    bps = l // t
    rb0 = row_off // t
    rb8 = row_off // SUBLANES
    t8 = t // SUBLANES
    j0 = col0 // tc
    return pl.pallas_call(
        functools.partial(_conv_kernel, blocks_per_seq=bps),
        grid=(b * bps, conv_dim // tc),
        in_specs=[pl.BlockSpec((t, tc), lambda r, j: (rb0 + r, j0 + j)),
                  pl.BlockSpec((SUBLANES, tc), lambda r, j: (jnp.maximum(rb8 + r * t8 - 1, 0), j0 + j)),
                  pl.BlockSpec((1, SUBLANES, tc), lambda r, j: (r // bps, 0, j)),
                  pl.BlockSpec((SSD_CONV, tc), lambda r, j: (0, j)),
                  pl.BlockSpec((1, tc), lambda r, j: (0, j))],
        out_specs=pl.BlockSpec((t, tc), lambda r, j: (r, j)),
        out_shape=jax.ShapeDtypeStruct((b * l, conv_dim), F32),
        compiler_params=_params(("parallel", "parallel")),
        name="ssd_conv",
    )(proj, proj, state8, conv_w, conv_b.reshape(1, conv_dim))


def _conv8_kernel(cur_ref, st_ref, w_ref, b_ref, o_ref):
    cur = cur_ref[...]
    n = cur.shape[0]
    st = st_ref[...].reshape(n, cur.shape[1])
    w = w_ref[...]
    rmod = jnp.bitwise_and(lax.broadcasted_iota(jnp.int32, cur.shape, 0), SUBLANES - 1)
    acc = b_ref[...] + cur * w[SSD_CONV - 1:SSD_CONV, :]
    for tap in range(SSD_CONV - 1):
        sh = SSD_CONV - 1 - tap
        shifted = jnp.where(rmod < sh, pltpu.roll(st, (n - SUBLANES + sh) % n, 0), pltpu.roll(cur, sh, 0))
        acc = acc + shifted * w[tap:tap + 1, :]
    o_ref[...] = _silu(acc)


def _ssd_conv8(proj, row_off, b, col0, conv_dim, state8, conv_w, conv_b):
    nb = _tile(b, 16, 1)
    t = nb * SUBLANES
    tc = _tile(math.gcd(col0, conv_dim), 2048, LANES)
    assert row_off % t == 0
    rb0 = row_off // t
    j0 = col0 // tc
    return pl.pallas_call(
        _conv8_kernel,
        grid=(b // nb, conv_dim // tc),
        in_specs=[pl.BlockSpec((t, tc), lambda r, j: (rb0 + r, j0 + j)),
                  pl.BlockSpec((nb, SUBLANES, tc), lambda r, j: (r, 0, j)),
                  pl.BlockSpec((SSD_CONV, tc), lambda r, j: (0, j)),
                  pl.BlockSpec((1, tc), lambda r, j: (0, j))],
        out_specs=pl.BlockSpec((t, tc), lambda r, j: (r, j)),
        out_shape=jax.ShapeDtypeStruct((b * SUBLANES, conv_dim), F32),
        compiler_params=_params(("parallel", "parallel")),
        name="ssd_conv8",
    )(proj, state8, conv_w, conv_b.reshape(1, conv_dim))


def _ssd_kernel(xs_ref, bm_ref, cm_ref, z_ref, dtc_ref, dtr_ref, biasc_ref, biasr_ref, alogc_ref, alogr_ref,
                dexp_ref, nw_ref, e_ref, s0_ref, y_ref, sout_ref, *scratch, nc, q, gb):
    c = pl.program_id(2)
    if nc > 1:
        s_scr, = scratch

        @pl.when(c == 0)
        def _():
            s_scr[...] = s0_ref[0]

    ii = lax.broadcasted_iota(jnp.int32, (q, q), 0)
    jj = lax.broadcasted_iota(jnp.int32, (q, q), 1)
    causal = ii >= jj
    lane =lax.broadcasted_iota(jnp.int32, (q, LANES), 1)
    e = e_ref[...]

    def expand(a):
        hi = a.astype(BF16)
        lo = (a - hi.astype(F32)).astype(BF16)
        return (jnp.dot(hi, e, preferred_element_type=F32) + jnp.dot(lo, e, preferred_element_type=F32))

    pending = []
    for gg in range(gb):
        gcols = slice(gg * SSD_GROUP_W, (gg + 1) * SSD_GROUP_W)
        ncols = slice(gg * SSD_STATE, (gg + 1) * SSD_STATE)
        dtc = _softplus(dtc_ref[gg, 0] + biasc_ref[gg])
        lac = dtc * (-jnp.exp(alogc_ref[gg]))
        dtr = _softplus(dtr_ref[gg, 0] + biasr_ref[gg])
        lar = dtr * (-jnp.exp(alogr_ref[gg]))
        cum_c = _cumsum_rows(lac)
        cum_r = _cumsum_cols(lar)
        last = cum_c[q - 1:q, :]

        xs = xs_ref[:, gcols]
        xdt = xs * expand(dtc)
        xdt_b = xdt.astype(BF16)
        bm = bm_ref[:, ncols].astype(BF16)
        cm = cm_ref[:, ncols].astype(BF16)
        cb = _nt(cm, bm)
        pairs = []
        for p in range(SSD_REP // 2):
            x2 = xdt_b[:, p * LANES:(p + 1) * LANES]
            ys = []
            for hh in (2 * p, 2 * p + 1):
                seg = cum_c[:, hh:hh + 1] - cum_r[hh:hh + 1, :]
                dec = jnp.where(causal, jnp.exp(jnp.where(causal, seg, 0.0)), 0.0)
                ys.append(jnp.dot((cb * dec).astype(BF16), x2, preferred_element_type=F32))
            pairs.append(jnp.where(lane < SSD_HEADDIM, ys[0], ys[1]))
        y = jnp.concatenate(pairs, axis=1)

        s = s_scr[gg] if nc > 1 else s0_ref[0, gg]
        y = y + _nt(cm, s.astype(BF16)) * expand(jnp.exp(cum_c))
        wj = jnp.exp(last - cum_c)
        xdtw = (xdt * expand(wj)).astype(BF16)
        y = y + xs * dexp_ref[gg]
        y = y * _silu(z_ref[:, gcols])
        y = y * lax.rsqrt(jnp.mean(y * y, -1, keepdims=True) + EPS)
        pending.append(((y * nw_ref[gg]).astype(y_ref.dtype), xdtw, bm, jnp.exp(last)))

    for gg, (y, xdtw, bm, elast) in enumerate(pending):
        y_ref[:, gg * SSD_GROUP_W:(gg + 1) * SSD_GROUP_W] = y
        s = s_scr[gg] if nc > 1 else s0_ref[0, gg]
        upd = _tn(xdtw, bm)
        dst = s_scr if nc > 1 else sout_ref.at[0]
        for hh in range(SSD_REP):
            rows = slice(hh * SSD_HEADDIM, (hh + 1) * SSD_HEADDIM)
            dst[gg, rows, :] = s[rows, :] * elast[:, hh:hh + 1] + upd[rows, :]

    if nc > 1:
        @pl.when(c == nc - 1)
        def _():
            sout_ref[0] = s_scr[...]


def _ssd_scan(proj, xact, dt_raw, row_off, b, l, q, gb, state, dt_bias, a_log, d_skip, norm_w, out_dtype):
    conv_dim = xact.shape[1]
    heads = dt_raw.shape[1]
    g = heads // SSD_REP
    dinner = heads * SSD_HEADDIM
    gn = g * SSD_STATE
    ngb = g // gb
    assert conv_dim == dinner + 2 * gn and l % q == 0 and row_off % q == 0 and g % gb == 0
    nc = l // q
    nblk = b * nc
    rb0 = row_off // q
    dtc = dt_raw.reshape(nblk, q, g, SSD_REP).transpose(2, 0, 1, 3)
    dtr = dt_raw.reshape(nblk, q, g, SSD_REP).transpose(2, 0, 3, 1)
    vec_c = lambda a: a.reshape(g, 1, SSD_REP)
    vec_r = lambda a: a.reshape(g, SSD_REP, 1)
    dexp = jnp.repeat(d_skip, SSD_HEADDIM).reshape(g, 1, SSD_GROUP_W)
    nw = norm_w.reshape(g, 1, SSD_GROUP_W)
    e = (jnp.arange(SSD_GROUP_W)[None, :] // SSD_HEADDIM == jnp.arange(SSD_REP)[:, None]).astype(BF16)
    st = state.reshape(b, g, SSD_GROUP_W, SSD_STATE)
    nb0 = dinner // (gb * SSD_STATE)

    blk = lambda bi, gi, ci: bi * nc + ci
    vc = pl.BlockSpec((gb, 1, SSD_REP), lambda bi, gi, ci: (gi, 0, 0))
    vr = pl.BlockSpec((gb, SSD_REP, 1), lambda bi, gi, ci: (gi, 0, 0))
    vg = pl.BlockSpec((gb, 1, SSD_GROUP_W), lambda bi, gi, ci: (gi, 0, 0))
    stspec = pl.BlockSpec((1, gb, SSD_GROUP_W, SSD_STATE), lambda bi, gi, ci: (bi, gi, 0, 0))
    y, s_out = pl.pallas_call(
        functools.partial(_ssd_kernel, nc=nc, q=q, gb=gb),
        grid=(b, ngb, nc),
        in_specs=[pl.BlockSpec((q, gb * SSD_GROUP_W), lambda bi, gi, ci: (blk(bi, gi, ci), gi)),
                  pl.BlockSpec((q, gb * SSD_STATE), lambda bi, gi, ci: (blk(bi, gi, ci), nb0 + gi)),
                  pl.BlockSpec((q, gb * SSD_STATE), lambda bi, gi, ci: (blk(bi, gi, ci), nb0 + ngb + gi)),
                  pl.BlockSpec((q, gb * SSD_GROUP_W), lambda bi, gi, ci: (rb0 + blk(bi, gi, ci), gi)),
                  pl.BlockSpec((gb, 1, q, SSD_REP), lambda bi, gi, ci: (gi, blk(bi, gi, ci), 0, 0)),
                  pl.BlockSpec((gb, 1, SSD_REP, q), lambda bi, gi, ci: (gi, blk(bi, gi, ci), 0, 0)),
                  vc, vr, vc, vr, vg, vg,
                  pl.BlockSpec((SSD_REP, SSD_GROUP_W), lambda bi, gi, ci: (0, 0)),
                  stspec],
        out_specs=[pl.BlockSpec((q, gb * SSD_GROUP_W), lambda bi, gi, ci: (blk(bi, gi, ci), gi)), stspec],
        out_shape=[jax.ShapeDtypeStruct((b * l, dinner), out_dtype),
                   jax.ShapeDtypeStruct(st.shape, F32)],
        scratch_shapes=[pltpu.VMEM((gb, SSD_GROUP_W, SSD_STATE), F32)] if nc > 1 else [],
        compiler_params=_params(("parallel", "parallel", "arbitrary")),
        name="ssd_scan",
    )(xact, xact, xact, proj, dtc, dtr, vec_c(dt_bias), vec_r(dt_bias), vec_c(a_log), vec_r(a_log),
      dexp, nw, e, st)
    return y, s_out.reshape(state.shape)


def _mlstm_kernel(q_ref, k_ref, v_ref, og_ref, gc_ref, gr_ref, bc_ref, br_ref, c0_ref, n0_ref, m0_ref,
                  h_ref, cout_ref, nout_ref, mout_ref, *scratch, nc, q, hb):
    c = pl.program_id(2)
    if nc > 1:
        c_scr, n_scr, m_scr = scratch

        @pl.when(c == 0)
        def _():
            c_scr[...] = c0_ref[0]
            n_scr[...] = n0_ref[0]
            m_scr[...] = m0_ref[0]

    ii = lax.broadcasted_iota(jnp.int32, (q, q), 0)
    jj = lax.broadcasted_iota(jnp.int32, (q, q), 1)
    causal = ii >= jj

    pending = []
    for hh in range(hb):
        qcols = slice(hh * ML_DQK, (hh + 1) * ML_DQK)
        vcols = slice(hh * ML_DV, (hh + 1) * ML_DV)
        gcol = gc_ref[hh, 0] + bc_ref[hh]
        grow = gr_ref[hh, 0] + br_ref[hh]
        ig_c = gcol[:, 0:1]
        ig_r = grow[0:1, :]
        lf_c = _log_sigmoid(gcol)
        lf_r = _log_sigmoid(grow)
        bcum_c = _cumsum_rows(lf_c)[:, 1:2]
        bcum_r = _cumsum_cols(lf_r)[1:2, :]

        if nc > 1:
            cs, ns, m_prev = c_scr[hh], n_scr[hh], m_scr[hh]
        else:
            cs, ns, m_prev = c0_ref[0, hh], n0_ref[0, hh], m0_ref[0, hh]
        logw = jnp.where(causal, bcum_c - bcum_r + ig_r, -jnp.inf)
        logs = bcum_c + m_prev
        mt = jnp.maximum(logs, jnp.max(logw, axis=1, keepdims=True))
        qf = q_ref[:, qcols]
        kf = k_ref[:, qcols] * (ML_DQK ** -0.5)
        qb = qf.astype(BF16)
        v = v_ref[:, vcols].astype(BF16)
        qk = _nt(qb, kf.astype(BF16)) * jnp.exp(logw - mt)
        ws = jnp.exp(logs - mt)
        num = (jnp.dot(qk.astype(BF16), v, preferred_element_type=F32)
               + jnp.dot(qb, cs.astype(BF16), preferred_element_type=F32) * ws)
        den = jnp.sum(qk, axis=1, keepdims=True) + jnp.sum(qf * ns, axis=1, keepdims=True) * ws
        hc = num / jnp.maximum(jnp.abs(den), jnp.exp(-mt))
        h_out = (jax.nn.sigmoid(og_ref[:, vcols]) * hc).astype(h_ref.dtype)

        m_new = mt[q - 1:q, :]
        blast = bcum_c[q - 1:q, :]
        wl = jnp.exp(blast - bcum_c + ig_c - m_new)
        sl = jnp.exp(blast + m_prev - m_new)
        kw = kf * wl
        n_new = ns * sl + jnp.sum(kw, axis=0, keepdims=True)
        pending.append((h_out, kw.astype(BF16), v, sl, n_new, m_new))

    for hh, (h_out, kw, v, sl, n_new, m_new) in enumerate(pending):
        h_ref[:, hh * ML_DV:(hh + 1) * ML_DV] = h_out
        cs = c_scr[hh] if nc > 1 else c0_ref[0, hh]
        c_new = cs * sl + _tn(kw, v)
        if nc > 1:
            c_scr[hh] = c_new
            n_scr[hh] = n_new
            m_scr[hh] = m_new

            @pl.when(c == nc - 1)
            def _():
                cout_ref[0, hh] = c_new
                nout_ref[0, hh] = n_new
                mout_ref[0, hh] = m_new
        else:
            cout_ref[0, hh] = c_new
            nout_ref[0, hh] = n_new
            mout_ref[0, hh] = m_new


def _mlstm(proj, graw, row_off, b, l, q, hb, c_state, n_state, m_state, b_gates, out_dtype):
    h = graw.shape[1] // 2
    d = h * ML_DV
    nc = l // q
    nblk = b * nc
    rb0 = row_off // q
    nhb = h // hb
    assert row_off % q == 0 and l % q == 0 and h % hb == 0
    g4 = graw.reshape(nblk, q, 2, h)
    gc = g4.transpose(3, 0, 1, 2)
    gr = g4.transpose(3, 0, 2, 1)
    bc = b_gates.reshape(2, h).T.reshape(h, 1, 2)
    br = b_gates.reshape(2, h).T.reshape(h, 2, 1)
    kb0 = nhb
    vb0 = (2 * h * ML_DQK) // (hb * ML_DV)
    ob0 = vb0 + nhb

    blk = lambda bi, hi, ci: bi * nc + ci
    cst = pl.BlockSpec((1, hb, ML_DQK, ML_DV), lambda bi, hi, ci: (bi, hi, 0, 0))
    nst = pl.BlockSpec((1, hb, 1, ML_DQK), lambda bi, hi, ci: (bi, hi, 0, 0))
    mst = pl.BlockSpec((1, hb, 1, 1), lambda bi, hi, ci: (bi, hi, 0, 0))
    scratch = []
    if nc > 1:
        scratch = [pltpu.VMEM((hb, ML_DQK, ML_DV), F32), pltpu.VMEM((hb, 1, ML_DQK), F32),
                   pltpu.VMEM((hb, 1, 1), F32)]
    hs, c_out, n_out, m_out = pl.pallas_call(
        functools.partial(_mlstm_kernel, nc=nc, q=q, hb=hb),
        grid=(b, nhb, nc),
        in_specs=[pl.BlockSpec((q, hb * ML_DQK), lambda bi, hi, ci: (rb0 + blk(bi, hi, ci), hi)),
                  pl.BlockSpec((q, hb * ML_DQK), lambda bi, hi, ci: (rb0 + blk(bi, hi, ci), kb0 + hi)),
                  pl.BlockSpec((q, hb * ML_DV), lambda bi, hi, ci: (rb0 + blk(bi, hi, ci), vb0 + hi)),
                  pl.BlockSpec((q, hb * ML_DV), lambda bi, hi, ci: (rb0 + blk(bi, hi, ci), ob0 + hi)),
                  pl.BlockSpec((hb, 1, q, 2), lambda bi, hi, ci: (hi, blk(bi, hi, ci), 0, 0)),
                  pl.BlockSpec((hb, 1, 2, q), lambda bi, hi, ci: (hi, blk(bi, hi, ci), 0, 0)),
                  pl.BlockSpec((hb, 1, 2), lambda bi, hi, ci: (hi, 0, 0)),
                  pl.BlockSpec((hb, 2, 1), lambda bi, hi, ci: (hi, 0, 0)),
                  cst, nst, mst],
        out_specs=[pl.BlockSpec((q, hb * ML_DV), lambda bi, hi, ci: (blk(bi, hi, ci), hi)), cst, nst, mst],
        out_shape=[jax.ShapeDtypeStruct((b * l, d), out_dtype),
                   jax.ShapeDtypeStruct((b, h, ML_DQK, ML_DV), F32),
                   jax.ShapeDtypeStruct((b, h, 1, ML_DQK), F32),
                   jax.ShapeDtypeStruct((b, h, 1, 1), F32)],
        scratch_shapes=scratch,
        compiler_params=_params(("parallel", "parallel", "arbitrary")),
        name="mlstm",
    )(proj, proj, proj, proj, gc, gr, bc, br, c_state, n_state.reshape(b, h, 1, ML_DQK),
      m_state.reshape(b, h, 1, 1))
    return hs, c_out, n_out.reshape(b, h, ML_DQK), m_out.reshape(b, h)


def _chunk(l, pref):
    return _tile(l, pref, SUBLANES)


def _pad_cols(w):
    return jnp.pad(w, ((0, 0), (0, -w.shape[1] % LANES)))


def kernel(x_prompt, x_sample, state_ret, state_ssm, state_conv, state_mlstm_C, state_mlstm_n, state_mlstm_m, norm_mix_pre, norm_mix_post, norm_ffn_pre, norm_ffn_post, ret_w_in, ret_w_out, ssd_w_in, ssd_conv_w, ssd_conv_b, ssd_dt_bias, ssd_a_log, ssd_d, ssd_norm_w, ssd_w_out, ml_w_in, ml_b_gates, ml_w_out, ffn_w_up, ffn_w_down):
    bp, lp, d = x_prompt.shape
    bs, ls, _ = x_sample.shape
    depth = norm_mix_pre.shape[0]
    mp, ms = bp * lp, bs * ls
    assert lp >= SSD_CONV - 1 and ls >= SSD_CONV - 1
    dinner = ssd_w_out.shape[1]
    ssd_heads = dinner // SSD_HEADDIM
    ssd_groups = ssd_heads // SSD_REP
    conv_dim = ssd_conv_w.shape[2]
    ml_heads = d // ML_DV
    ml_qk = ml_heads * ML_DQK
    ret_heads = d // RET_DK
    n_ret = state_ret.shape[0]
    qs = _chunk(ls, 256)
    dt = x_prompt.dtype

    x_parts = (x_prompt.reshape(mp, d), x_sample.reshape(ms, d))
    hn = _prenorm2(*x_parts, norm_mix_pre[0])
    new = {k: ([], []) for k in ("ssm", "conv", "c", "n", "m")}
    ret_p, ret_s, ret_deferred = [], None, []
    zero_ret = jnp.zeros((1, bp, ret_heads, RET_DK, RET_DK), F32)

    in_stacks = (ret_w_in, ssd_w_in, ml_w_in)
    out_stacks = (ret_w_out, ssd_w_out, ml_w_out)
    w_in = _to_bf16(in_stacks[0], 0)

    for i in range(depth):
        j = i // 3
        kind = i % 3
        side_in = ((ffn_w_up, i), (out_stacks[kind], j))
        if kind == 0:
            proj, (w_up, w_out) = _matmul(hn, w_in, side=side_in)
            op, sp = _retention(proj, 0, bp, lp, _chunk(lp, 256), _tile(ret_heads, 4, 1), 0.0, zero_ret, 0, BF16)
            ret_p.append(sp[0])
            if j + 1 < n_ret:
                os_, _ = _retention(proj, mp, bs, ls, qs, ret_heads, float(PAST_LEN), state_ret, j, BF16,
                                    emit_state=False)
                ret_deferred.append((proj, j))
            else:
                os_, ret_s = _retention(proj, mp, bs, ls, qs, ret_heads, float(PAST_LEN), state_ret, j, BF16,
                                        deferred=tuple(ret_deferred))
        elif kind == 1:
            main = dinner + conv_dim
            proj, (w_up, w_out) = _matmul(hn, w_in, n_cols=main, side=side_in)
            dt_raw = _matmul(hn, _pad_cols(w_in[:, main:]))[:, :ssd_heads]
            outs = []
            for g_i, (off, b, l, q, gb, st_ssm, st_conv) in enumerate((
                    (0, bp, lp, _chunk(lp, 128), _tile(ssd_groups, 2, 1), jnp.zeros((bp,) + state_ssm.shape[2:], F32),
                     jnp.zeros((bp, SSD_CONV - 1, conv_dim), F32)),
                    (mp, bs, ls, qs, ssd_groups, state_ssm[j], state_conv[j]))):
                st8 = jnp.pad(st_conv, ((0, 0), (SUBLANES - (SSD_CONV - 1), 0), (0, 0)))
                if l == SUBLANES:
                    xact = _ssd_conv8(proj, off, b, dinner, conv_dim, st8, ssd_conv_w[j], ssd_conv_b[j])
                else:
                    xact = _ssd_conv(proj, off, b, l, dinner, conv_dim, st8, ssd_conv_w[j], ssd_conv_b[j])
                y, s_new = _ssd_scan(proj, xact, dt_raw[off:off + b * l], off, b, l, q, gb, st_ssm,
                                     ssd_dt_bias[j], ssd_a_log[j], ssd_d[j], ssd_norm_w[j], BF16)
                outs.append(y)
                new["ssm"][g_i].append(s_new)
                tails = [lax.slice(proj, (off + (bi + 1) * l - (SSD_CONV - 1), dinner), (off + (bi + 1) * l, main))
                         for bi in range(b)] if b <= 8 else None
                if tails is None:
                    tails = lax.slice(proj, (off, dinner), (off + b * l, main)).reshape(b, l, conv_dim)
                    new["conv"][g_i].append(tails[:, l - (SSD_CONV - 1):])
                else:
                    new["conv"][g_i].append(jnp.stack(tails))
            op, os_ = outs
        else:
            main = 2 * ml_qk + 2 * d
            proj, (w_up, w_out) = _matmul(hn, w_in, n_cols=main, side=side_in)
            graw = _matmul(hn, _pad_cols(w_in[:, main:]))[:, :2 * ml_heads]
            op, cp, np_, mp_ = _mlstm(proj, graw[:mp], 0, bp, lp, _chunk(lp, 128), _tile(ml_heads, 2, 1),
                                      jnp.zeros((bp,) + state_mlstm_C.shape[2:], F32),
                                      jnp.zeros((bp,) + state_mlstm_n.shape[2:], F32),
                                      jnp.zeros((bp,) + state_mlstm_m.shape[2:], F32), ml_b_gates[j], BF16)
            os_, cs_, ns_, ms_ = _mlstm(proj, graw[mp:], mp, bs, ls, qs, ml_heads, state_mlstm_C[j],
                                        state_mlstm_n[j], state_mlstm_m[j], ml_b_gates[j], BF16)
            for key, vp, vs in (("c", cp, cs_), ("n", np_, ns_), ("m", mp_, ms_)):
                new[key][0].append(vp)
                new[key][1].append(vs)

        out = _matmul(op, w_out, a2=os_, out_dtype=BF16)
        x, hn = _post_residual_prenorm(out, x_parts, norm_mix_post[i], norm_ffn_pre[i])
        up, (w_down,) = _matmul(hn, w_up, act="relu2", out_dtype=BF16, side=((ffn_w_down, i),))
        if i + 1 < depth:
            nxt = ((in_stacks[(i + 1) % 3], (i + 1) // 3),)
            f, (w_in,) = _matmul(up, w_down, out_dtype=BF16, side=nxt)
            x, hn = _post_residual_prenorm(f, (x,), norm_ffn_post[i], norm_mix_pre[i + 1])
            x_parts = (x,)
        else:
            f = _matmul(up, w_down, out_dtype=BF16)
            y_prompt, _ = _post_residual_prenorm(f, (x,), norm_ffn_post[i], None, 0, mp)
            y_sample, _ = _post_residual_prenorm(f, (x,), norm_ffn_post[i], None, mp, ms)

    st = lambda key, g_i: jnp.stack(new[key][g_i]).astype(dt)
    return (y_prompt.reshape(bp, lp, d), y_sample.reshape(bs, ls, d), jnp.stack(ret_p).astype(dt), ret_s.astype(dt),
            st("ssm", 0), st("ssm", 1), st("conv", 0), st("conv", 1), st("c", 0), st("c", 1),
            st("n", 0), st("n", 1), st("m", 0), st("m", 1))
```
